```python
import math
import jax, jax.numpy as jnp
from jax import lax
import numpy as np

D_MODEL = 1024
BATCH = 8
SEQ = 8192
DEPTH = 2

CHUNK = 64
D_MIX = D_MODEL
POOL_WIDTH = D_MIX // 4
DIFF_WIDTH = D_MIX // 2
GLA_WIDTH = D_MIX // 4
POOL_WINDOWS = (2, 4, 8, 16)
POOL_GROUPS = len(POOL_WINDOWS)
POOL_GDIM = POOL_WIDTH // POOL_GROUPS
DIFF_HEADS = 4
DIFF_VDIM = DIFF_WIDTH // DIFF_HEADS
DIFF_QKDIM = DIFF_VDIM // 2
GLA_HEADS = 4
GLA_VDIM = GLA_WIDTH // GLA_HEADS
GLA_KDIM = GLA_VDIM // 2
GLA_GATE_RANK = 16
GLA_GATE_TAU = 16.0
D_FF = 4 * D_MODEL
Q_BLOCK = 128
EPS = 1e-6

SIZES = (
    POOL_WIDTH,
    DIFF_HEADS * 2 * DIFF_QKDIM,
    DIFF_HEADS * 2 * DIFF_QKDIM,
    DIFF_HEADS * DIFF_VDIM,
    GLA_HEADS * GLA_KDIM,
    GLA_HEADS * GLA_KDIM,
    GLA_WIDTH,
    GLA_WIDTH,
    GLA_GATE_RANK,
)
D_IN_PROJ = sum(SIZES)
SPLIT_POINTS = tuple(sum(SIZES[:i + 1]) for i in range(len(SIZES) - 1))

kernel_name = "hybrid_pool_diffattn_gla_trunk"


def rms_norm(x, g):
    xf = x.astype(jnp.float32)
    y = xf * lax.rsqrt(jnp.mean(xf * xf, axis=-1, keepdims=True) + EPS)
    return (y * g.astype(jnp.float32)).astype(x.dtype)


def pool_mixer(u, w_pool, pool_scale):
    B, S, _ = u.shape
    uf = u.astype(jnp.float32).reshape(B, S, POOL_GROUPS, POOL_GDIM)
    csum = jnp.cumsum(uf, axis=1)
    t = jnp.arange(S)
    means = []
    for g, win in enumerate(POOL_WINDOWS):
        c = csum[:, :, g]
        c_prev = jnp.pad(c, ((0, 0), (win, 0), (0, 0)))[:, :S]
        cnt = jnp.minimum(t + 1, win).astype(jnp.float32)[None, :, None]
        means.append((c - c_prev) / cnt)
    pooled = jnp.stack(means, axis=2) - uf
    mixed = jnp.einsum('bsgc,gcd->bsgd', pooled, w_pool.astype(jnp.float32))
    return (mixed.reshape(B, S, POOL_WIDTH) * pool_scale.astype(jnp.float32)).astype(u.dtype)


def diff_attention(q, k, v, lq1, lk1, lq2, lk2, norm_g, lambda_init):
    B, S = q.shape[:2]
    nqb = S // Q_BLOCK
    f32 = jnp.float32
    lam = (jnp.exp(jnp.sum(lq1.astype(f32) * lk1.astype(f32)))
           - jnp.exp(jnp.sum(lq2.astype(f32) * lk2.astype(f32))) + lambda_init)
    scale = DIFF_QKDIM ** -0.5
    k_chunk = jnp.arange(S) // CHUNK
    qb = q.reshape(B, nqb, Q_BLOCK, DIFF_HEADS, 2, DIFF_QKDIM).transpose(1, 0, 2, 3, 4, 5)

    def one_block(args):
        q_blk, blk = args
        s = jnp.einsum('bqhmd,bkhmd->bhmqk', q_blk, k).astype(f32) * scale
        q_chunk = (blk * Q_BLOCK + jnp.arange(Q_BLOCK)) // CHUNK
        mask = k_chunk[None, :] <= q_chunk[:, None]
        s = jnp.where(mask, s, -jnp.inf)
        p = jax.nn.softmax(s, axis=-1)
        a = p[:, :, 0] - lam * p[:, :, 1]
        return jnp.einsum('bhqk,bkhd->bqhd', a.astype(v.dtype), v)

    o = lax.map(one_block, (qb, jnp.arange(nqb)))
    o = o.transpose(1, 0, 2, 3, 4).reshape(B, S, DIFF_HEADS, DIFF_VDIM)
    o = rms_norm(o, norm_g.reshape(DIFF_HEADS, DIFF_VDIM)) * (1.0 - lambda_init)
    return o.reshape(B, S, DIFF_WIDTH)


def gla_mixer(q, k, v, r, g_lr, w_gate2, b_gate, norm_g):
    B, S = q.shape[:2]
    nc = S // CHUNK
    f32 = jnp.float32
    log_a = jax.nn.log_sigmoid((g_lr @ w_gate2 + b_gate).astype(f32)) / GLA_GATE_TAU
    log_a = log_a.reshape(B, S, GLA_HEADS, GLA_KDIM)

    def to_chunks(t):
        return t.astype(f32).reshape(B, nc, CHUNK, GLA_HEADS, t.shape[-1]).transpose(1, 0, 3, 2, 4)

    qc = to_chunks(q) * (GLA_KDIM ** -0.5)
    kc, vc, lc = to_chunks(k), to_chunks(v), to_chunks(log_a)
    bcum = jnp.cumsum(lc, axis=3)
    b_last = bcum[:, :, :, -1, :]
    q_dec = qc * jnp.exp(bcum)
    k_inv = kc * jnp.exp(-bcum)
    k_end = kc * jnp.exp(b_last[:, :, :, None, :] - bcum)
    causal = jnp.tril(jnp.ones((CHUNK, CHUNK), dtype=bool))
    att = jnp.where(causal, jnp.einsum('nbhid,nbhjd->nbhij', q_dec, k_inv), 0.0)
    o_intra = jnp.einsum('nbhij,nbhje->nbhie', att, vc)

    def step(state, xs):
        q_d, k_e, v_c, decay = xs
        o_inter = jnp.einsum('bhid,bhde->bhie', q_d, state)
        state = state * jnp.exp(decay)[..., None] + jnp.einsum('bhjd,bhje->bhde', k_e, v_c)
        return state, o_inter

    state0 = jnp.zeros((B, GLA_HEADS, GLA_KDIM, GLA_VDIM), f32)
    _, o_inter = lax.scan(step, state0, (q_dec, k_end, vc, b_last))
    o = (o_intra + o_inter).transpose(1, 0, 3, 2, 4).reshape(B, S, GLA_HEADS, GLA_VDIM)
    o = rms_norm(o, norm_g.reshape(GLA_HEADS, GLA_VDIM)).reshape(B, S, GLA_WIDTH)
    return (o * jax.nn.silu(r.astype(f32))).astype(r.dtype)


def setup_inputs(seed: int = 0) -> dict:
    key = jax.random.key(seed)
    ks = jax.random.split(key, 24)
    nrm = jax.random.normal
    f32 = jnp.float32
    L = DEPTH
    return {
        "x": nrm(ks[0], (BATCH, SEQ, D_MODEL), f32),
        "norm1_g": 1.0 + 0.05 * nrm(ks[1], (L, D_MODEL), f32),
        "w_in": nrm(ks[2], (L, D_MODEL, D_IN_PROJ), f32) * D_MODEL ** -0.5,
        "pool_w": nrm(ks[3], (L, POOL_GROUPS, POOL_GDIM, POOL_GDIM), f32) * POOL_GDIM ** -0.5,
        "pool_scale": 1.0 + 0.1 * nrm(ks[4], (L, POOL_WIDTH), f32),
        "diff_lq1": 0.1 * nrm(ks[5], (L, DIFF_QKDIM), f32),
        "diff_lk1": 0.1 * nrm(ks[6], (L, DIFF_QKDIM), f32),
        "diff_lq2": 0.1 * nrm(ks[7], (L, DIFF_QKDIM), f32),
        "diff_lk2": 0.1 * nrm(ks[8], (L, DIFF_QKDIM), f32),
        "diff_norm_g": 1.0 + 0.05 * nrm(ks[9], (L, DIFF_WIDTH), f32),
        "gla_w_gate2": nrm(ks[10], (L, GLA_GATE_RANK, GLA_HEADS * GLA_KDIM), f32) * GLA_GATE_RANK ** -0.5,
        "gla_b_gate": 0.1 * nrm(ks[11], (L, GLA_HEADS * GLA_KDIM), f32),
        "gla_norm_g": 1.0 + 0.05 * nrm(ks[12], (L, GLA_WIDTH), f32),
        "w_out": nrm(ks[13], (L, D_MIX, D_MODEL), f32) * D_MIX ** -0.5,
        "norm2_g": 1.0 + 0.05 * nrm(ks[14], (L, D_MODEL), f32),
        "w_mlp1": nrm(ks[15], (L, D_MODEL, D_FF), f32) * D_MODEL ** -0.5,
        "w_mlp2": nrm(ks[16], (L, D_FF, D_MODEL), f32) * D_FF ** -0.5,
        "final_norm_g": 1.0 + 0.05 * nrm(ks[17], (D_MODEL,), f32),
    }


def reference(x, norm1_g, w_in, pool_w, pool_scale, diff_lq1, diff_lk1, diff_lq2, diff_lk2,
              diff_norm_g, gla_w_gate2, gla_b_gate, gla_norm_g, w_out, norm2_g, w_mlp1,
              w_mlp2, final_norm_g):
    B, S, _ = x.shape
    h = x
    for l in range(DEPTH):
        u = rms_norm(h, norm1_g[l])
        proj = u @ w_in[l]
        p_in, dq, dk, dv, gq, gk, gv, gr, gg = jnp.split(proj, SPLIT_POINTS, axis=-1)
        y_pool = pool_mixer(p_in, pool_w[l], pool_scale[l])
        lambda_init = 0.8 - 0.6 * math.exp(-0.3 * l)
        y_diff = diff_attention(
            dq.reshape(B, S, DIFF_HEADS, 2, DIFF_QKDIM),
            dk.reshape(B, S, DIFF_HEADS, 2, DIFF_QKDIM),
            dv.reshape(B, S, DIFF_HEADS, DIFF_VDIM),
            diff_lq1[l], diff_lk1[l], diff_lq2[l], diff_lk2[l], diff_norm_g[l], lambda_init)
        y_gla = gla_mixer(
            gq.reshape(B, S, GLA_HEADS, GLA_KDIM),
            gk.reshape(B, S, GLA_HEADS, GLA_KDIM),
            gv.reshape(B, S, GLA_HEADS, GLA_VDIM),
            gr, gg, gla_w_gate2[l], gla_b_gate[l], gla_norm_g[l])
        mix = jnp.concatenate([y_pool, y_diff.astype(y_pool.dtype), y_gla.astype(y_pool.dtype)], axis=-1)
        h = h + (mix @ w_out[l]).astype(h.dtype)
        z = rms_norm(h, norm2_g[l])
        h = h + jnp.square(jax.nn.relu(z @ w_mlp1[l])) @ w_mlp2[l]
    return rms_norm(h, final_norm_g)
```

```python
import functools
import math

import jax
import jax.numpy as jnp
from jax import lax
from jax.experimental import pallas as pl
from jax.experimental.pallas import tpu as pltpu

F32 = jnp.float32
BF16 = jnp.bfloat16

D_MODEL = 1024
CHUNK = 64
POOL_WIDTH = 256
POOL_WINDOWS = (2, 4, 8, 16)
POOL_GDIM = 64
POOL_HALO = 16
DIFF_HEADS = 4
DIFF_VDIM = 128
DIFF_QKDIM = 64
DIFF_WIDTH = 512
GLA_HEADS = 4
GLA_VDIM = 64
GLA_KDIM = 32
GLA_WIDTH = 256
GLA_KW = GLA_HEADS * GLA_KDIM
GLA_GATE_RANK = 16
GLA_GATE_TAU = 16.0
D_FF = 4096
EPS = 1e-6

LANES = 128
QKV_W = 3 * DIFF_WIDTH
REST_POOL = 0
REST_GQ = 256
REST_GK = 384
REST_GV = 512
REST_GR = 768
REST_GG = 1024
REST_W = 1152
W_IN_COLS = QKV_W + REST_W

VMEM_LIMIT = 56 * 1024 * 1024
NEG_BIG = -1e30


def _resident(shape):
    nd = len(shape)
    return pl.BlockSpec(shape, lambda *_: (0,) * nd, pipeline_mode=pl.Buffered(1))


def _split_bf16(a):
    hi = a.astype(BF16)
    lo = (a - hi.astype(F32)).astype(BF16)
    return hi, lo


def _inproj_kernel(h_ref, g_ref, w_ref, qkv_ref, rest_ref):
    h = h_ref[...]
    ms = jnp.mean(h * h, axis=-1, keepdims=True)
    u = (h * lax.rsqrt(ms + EPS) * g_ref[...]).astype(BF16)
    qkv_ref[...] = jnp.dot(u, w_ref[:, :QKV_W], preferred_element_type=F32).astype(BF16)
    rest_ref[...] = jnp.dot(u, w_ref[:, QKV_W:], preferred_element_type=F32)


def _inproj(h, g, w, tm):
    t = h.shape[0]
    return pl.pallas_call(
        _inproj_kernel,
        grid=(t // tm,),
        in_specs=[
            pl.BlockSpec((tm, D_MODEL), lambda i: (i, 0)),
            _resident((1, D_MODEL)),
            _resident((D_MODEL, W_IN_COLS)),
        ],
        out_specs=[
            pl.BlockSpec((tm, QKV_W), lambda i: (i, 0)),
            pl.BlockSpec((tm, REST_W), lambda i: (i, 0)),
        ],
        out_shape=[
            jax.ShapeDtypeStruct((t, QKV_W), BF16),
            jax.ShapeDtypeStruct((t, REST_W), F32),
        ],
        compiler_params=pltpu.CompilerParams(
            dimension_semantics=("parallel",), vmem_limit_bytes=VMEM_LIMIT),
        name="inproj",
    )(h, g, w)


def _attn_kernel(q_ref, k_ref, v_ref, lam_ref, g_ref, o_ref,
                 qs_scr, m_scr, l_scr, acc_scr, *, tq, lambda_init):
    qi = pl.program_id(2)
    m_rows = 2 * tq

    q = q_ref[...]
    lane = lax.broadcasted_iota(jnp.int32, (tq, LANES), 1)
    zero = jnp.zeros_like(q)
    qs_scr[0:tq, :] = jnp.where(lane < DIFF_QKDIM, q, zero)
    qs_scr[tq:m_rows, :] = jnp.where(lane >= DIFF_QKDIM, q, zero)

    def scores(start):
        k = k_ref[pl.ds(start, tq), :]
        return lax.dot_general(qs_scr[...], k, (((1,), (1,)), ((), ())),
                               preferred_element_type=F32)

    diag = pl.multiple_of(qi * tq, tq)
    s = scores(diag)
    row = lax.broadcasted_iota(jnp.int32, (m_rows, tq), 0)
    col = lax.broadcasted_iota(jnp.int32, (m_rows, tq), 1)
    q_chunk = jnp.where(row >= tq, row - tq, row) // CHUNK
    s = jnp.where(col // CHUNK <= q_chunk, s, NEG_BIG)
    m0 = jnp.max(s, axis=1, keepdims=True)
    p = jnp.exp(s - m0)
    m_scr[...] = jnp.broadcast_to(m0, (m_rows, LANES))
    l_scr[...] = jnp.broadcast_to(jnp.sum(p, axis=1, keepdims=True), (m_rows, LANES))
    acc_scr[...] = jnp.dot(p.astype(BF16), v_ref[pl.ds(diag, tq), :],
                           preferred_element_type=F32)

    def body(j, carry):
        start = pl.multiple_of(j * tq, tq)
        s = scores(start)
        m_prev = m_scr[...]
        m_next = jnp.maximum(m_prev, jnp.max(s, axis=1, keepdims=True))
        alpha = jnp.exp(m_prev - m_next)
        p = jnp.exp(s - m_next[:, 0:1])
        l_scr[...] = alpha * l_scr[...] + jnp.sum(p, axis=1, keepdims=True)
        m_scr[...] = m_next
        acc_scr[...] = alpha * acc_scr[...] + jnp.dot(
            p.astype(BF16), v_ref[pl.ds(start, tq), :], preferred_element_type=F32)
        return carry

    lax.fori_loop(0, qi, body, 0)

    out = acc_scr[...] / l_scr[...]
    o = out[0:tq] - lam_ref[...] * out[tq:m_rows]
    ms = jnp.mean(o * o, axis=-1, keepdims=True)
    o = o * lax.rsqrt(ms + EPS) * g_ref[...] * (1.0 - lambda_init)
    o_ref[...] = o.astype(o_ref.dtype)


def _lambda_kernel(lq1_ref, lk1_ref, lq2_ref, lk2_ref, lam_ref, *, lambda_init):
    a = jnp.sum(lq1_ref[...] * lk1_ref[...], axis=-1, keepdims=True)
    b = jnp.sum(lq2_ref[...] * lk2_ref[...], axis=-1, keepdims=True)
    lam_ref[...] = jnp.broadcast_to(jnp.exp(a) - jnp.exp(b) + lambda_init, lam_ref.shape)


def _diff_lambda(lq1, lk1, lq2, lk2, lambda_init):
    args = [a.reshape(1, DIFF_QKDIM) for a in (lq1, lk1, lq2, lk2)]
    return pl.pallas_call(
        functools.partial(_lambda_kernel, lambda_init=lambda_init),
        out_shape=jax.ShapeDtypeStruct((1, LANES), F32),
        name="diff_lambda",
    )(*args)


def _diff_attention(qkv, lam, norm_g, batch, seq, tq, lambda_init):
    t = batch * seq
    nq = seq // tq
    kcol = DIFF_WIDTH // DIFF_VDIM
    vcol = 2 * DIFF_WIDTH // DIFF_VDIM
    return pl.pallas_call(
        functools.partial(_attn_kernel, tq=tq, lambda_init=lambda_init),
        grid=(batch, DIFF_HEADS, nq),
        in_specs=[
            pl.BlockSpec((tq, DIFF_VDIM), lambda b, h, i: (b * nq + i, h)),
            pl.BlockSpec((seq, DIFF_VDIM), lambda b, h, i: (b, kcol + h)),
            pl.BlockSpec((seq, DIFF_VDIM), lambda b, h, i: (b, vcol + h)),
            pl.BlockSpec((1, LANES), lambda b, h, i: (0, 0)),
            pl.BlockSpec((1, DIFF_VDIM), lambda b, h, i: (0, h)),
        ],
        out_specs=pl.BlockSpec((tq, DIFF_VDIM), lambda b, h, i: (b * nq + i, h)),
        out_shape=jax.ShapeDtypeStruct((t, DIFF_WIDTH), BF16),
        scratch_shapes=[
            pltpu.VMEM((2 * tq, LANES), BF16),
            pltpu.VMEM((2 * tq, LANES), F32),
            pltpu.VMEM((2 * tq, LANES), F32),
            pltpu.VMEM((2 * tq, DIFF_VDIM), F32),
        ],
        compiler_params=pltpu.CompilerParams(
            dimension_semantics=("parallel", "parallel", "arbitrary"),
            vmem_limit_bytes=VMEM_LIMIT),
        name="diff_attention",
    )(qkv, qkv, qkv, lam, norm_g)


def _gla_pool_kernel(rest_ref, wg2_ref, bg_ref, gng_ref, pw_ref, ps_ref, out_ref,
                     state_scr, halo_scr, *, ts):
    st = pl.program_id(1)
    nchunk = ts // CHUNK

    @pl.when(st == 0)
    def _():
        state_scr[...] = jnp.zeros_like(state_scr)
        halo_scr[0:POOL_HALO, :] = jnp.zeros((POOL_HALO, POOL_WIDTH), F32)

    x = rest_ref[:, REST_POOL:REST_POOL + POOL_WIDTH]
    halo_scr[POOL_HALO:POOL_HALO + ts, :] = x

    def shifted(k, lo, hi):
        return halo_scr[POOL_HALO - k:POOL_HALO - k + ts, lo:hi]

    xa = x[:, 0:LANES]
    c2 = xa + shifted(1, 0, LANES)
    c4 = c2 + shifted(2, 0, LANES) + shifted(3, 0, LANES)
    xb = x[:, LANES:2 * LANES]
    c8 = xb
    for k in range(1, 8):
        c8 = c8 + shifted(k, LANES, 2 * LANES)
    c16 = c8
    for k in range(8, 16):
        c16 = c16 + shifted(k, LANES, 2 * LANES)
    halo_scr[0:POOL_HALO, :] = halo_scr[ts:ts + POOL_HALO, :]

    tpos = (st * ts + lax.broadcasted_iota(jnp.int32, (ts, LANES), 0) + 1).astype(F32)
    lane = lax.broadcasted_iota(jnp.int32, (ts, LANES), 1)
    first = lane < POOL_GDIM
    mean_a = jnp.where(first, c2 / jnp.minimum(tpos, 2.0), c4 / jnp.minimum(tpos, 4.0))
    mean_b = jnp.where(first, c8 / jnp.minimum(tpos, 8.0), c16 / jnp.minimum(tpos, 16.0))
    pooled = jnp.concatenate([mean_a - xa, mean_b - xb], axis=1).astype(BF16)
    y_pool = jnp.dot(pooled, pw_ref[...], preferred_element_type=F32) * ps_ref[...]
    out_ref[:, 0:POOL_WIDTH] = y_pool.astype(out_ref.dtype)

    gq = rest_ref[:, REST_GQ:REST_GQ + GLA_KW]
    gk = rest_ref[:, REST_GK:REST_GK + GLA_KW]
    gv = rest_ref[:, REST_GV:REST_GV + GLA_WIDTH]
    gr = rest_ref[:, REST_GR:REST_GR + GLA_WIDTH]
    gg = rest_ref[:, REST_GG:REST_GG + LANES]

    g_hi, g_lo = _split_bf16(gg)
    w_hi, w_lo = _split_bf16(wg2_ref[...])
    xg = (jnp.dot(g_hi, w_hi, preferred_element_type=F32)
          + jnp.dot(g_lo, w_hi, preferred_element_type=F32)
          + jnp.dot(g_hi, w_lo, preferred_element_type=F32)) + bg_ref[...]
    log_a = (jnp.minimum(xg, 0.0) - jnp.log1p(jnp.exp(-jnp.abs(xg)))) * (1.0 / GLA_GATE_TAU)

    r = lax.broadcasted_iota(jnp.int32, (ts, ts), 0)
    c = lax.broadcasted_iota(jnp.int32, (ts, ts), 1)
    same_chunk = (r // CHUNK) == (c // CHUNK)
    causal = jnp.logical_and(same_chunk, c <= r)
    tri = jnp.where(causal, 1.0, 0.0).astype(BF16)
    blk = jnp.where(same_chunk, 1.0, 0.0).astype(BF16)
    la_hi, la_lo = _split_bf16(log_a)
    bcum = (jnp.dot(tri, la_hi, preferred_element_type=F32)
            + jnp.dot(tri, la_lo, preferred_element_type=F32))
    btot = (jnp.dot(blk, la_hi, preferred_element_type=F32)
            + jnp.dot(blk, la_lo, preferred_element_type=F32))

    q_dec = (gq * (GLA_KDIM ** -0.5)) * jnp.exp(bcum)
    k_inv = (gk * jnp.exp(-bcum)).astype(BF16)
    k_end = (gk * jnp.exp(btot - bcum)).astype(BF16)
    decay = jnp.exp(btot)
    gv_b = gv.astype(BF16)

    klane = lax.broadcasted_iota(jnp.int32, (ts, GLA_KW), 1)
    vlane = lax.broadcasted_iota(jnp.int32, (ts, GLA_WIDTH), 1)
    q_dec_b = q_dec.astype(BF16)
    o = jnp.zeros((ts, GLA_WIDTH), F32)
    for h in range(GLA_HEADS):
        qh = jnp.where(klane // GLA_KDIM == h, q_dec_b, jnp.zeros_like(q_dec_b))
        att = lax.dot_general(qh, k_inv, (((1,), (1,)), ((), ())), preferred_element_type=F32)
        att = jnp.where(causal, att, 0.0).astype(BF16)
        vh = jnp.where(vlane // GLA_VDIM == h, gv_b, jnp.zeros_like(gv_b))
        o = o + jnp.dot(att, vh, preferred_element_type=F32)

    srow = lax.broadcasted_iota(jnp.int32, (GLA_WIDTH, GLA_KW), 0)
    scol = lax.broadcasted_iota(jnp.int32, (GLA_WIDTH, GLA_KW), 1)
    head_diag = (srow // GLA_VDIM) == (scol // GLA_KDIM)
    state = state_scr[...]
    o_inter = []
    for n in range(nchunk):
        sl = slice(n * CHUNK, (n + 1) * CHUNK)
        o_inter.append(lax.dot_general(q_dec_b[sl], state.astype(BF16), (((1,), (1,)), ((), ())),
                                       preferred_element_type=F32))
        kv_t = lax.dot_general(gv_b[sl], k_end[sl], (((0,), (0,)), ((), ())),
                               preferred_element_type=F32)
        state = state * decay[n * CHUNK:n * CHUNK + 1, :] + jnp.where(head_diag, kv_t, 0.0)
    state_scr[...] = state
    o = o + jnp.concatenate(o_inter, axis=0)

    hr = lax.broadcasted_iota(jnp.int32, (GLA_WIDTH, GLA_WIDTH), 0)
    hc = lax.broadcasted_iota(jnp.int32, (GLA_WIDTH, GLA_WIDTH), 1)
    ones_blk = jnp.where((hr // GLA_VDIM) == (hc // GLA_VDIM), 1.0, 0.0).astype(BF16)
    sq_hi, sq_lo = _split_bf16(o * o)
    ms = (jnp.dot(sq_hi, ones_blk, preferred_element_type=F32)
          + jnp.dot(sq_lo, ones_blk, preferred_element_type=F32)) * (1.0 / GLA_VDIM)
    o = o * lax.rsqrt(ms + EPS) * gng_ref[...]
    silu = gr / (1.0 + jnp.exp(-gr))
    out_ref[:, POOL_WIDTH:POOL_WIDTH + GLA_WIDTH] = (o * silu).astype(out_ref.dtype)


def _gla_pool(rest, wg2, bg, gng, pw, ps, batch, seq, ts):
    t = batch * seq
    ns = seq // ts
    return pl.pallas_call(
        functools.partial(_gla_pool_kernel, ts=ts),
        grid=(batch, ns),
        in_specs=[
            pl.BlockSpec((ts, REST_W), lambda b, s: (b * ns + s, 0)),
            _resident((LANES, GLA_KW)),
            _resident((1, GLA_KW)),
            _resident((1, GLA_WIDTH)),
            _resident((POOL_WIDTH, POOL_WIDTH)),
            _resident((1, POOL_WIDTH)),
        ],
        out_specs=pl.BlockSpec((ts, POOL_WIDTH + GLA_WIDTH), lambda b, s: (b * ns + s, 0)),
        out_shape=jax.ShapeDtypeStruct((t, POOL_WIDTH + GLA_WIDTH), BF16),
        scratch_shapes=[
            pltpu.VMEM((GLA_WIDTH, GLA_KW), F32),
            pltpu.VMEM((POOL_HALO + ts, POOL_WIDTH), F32),
        ],
        compiler_params=pltpu.CompilerParams(
            dimension_semantics=("parallel", "arbitrary"), vmem_limit_bytes=VMEM_LIMIT),
        name="gla_pool",
    )(rest, wg2, bg, gng, pw, ps)


def _out_mlp_kernel(h_ref, ypg_ref, yd_ref, wo_ref, g2_ref, w1_ref, w2_ref, gf_ref, o_ref,
                    *, ffc, final):
    npg = POOL_WIDTH + GLA_WIDTH
    h1 = (h_ref[...]
          + jnp.dot(ypg_ref[...], wo_ref[0:npg, :], preferred_element_type=F32)
          + jnp.dot(yd_ref[...], wo_ref[npg:D_MODEL, :], preferred_element_type=F32))
    ms = jnp.mean(h1 * h1, axis=-1, keepdims=True)
    z = (h1 * lax.rsqrt(ms + EPS) * g2_ref[...]).astype(BF16)
    del ffc
    a = jnp.maximum(jnp.dot(z, w1_ref[...], preferred_element_type=F32), 0.0)
    acc = h1 + jnp.dot((a * a).astype(BF16), w2_ref[...], preferred_element_type=F32)
    if final:
        ms = jnp.mean(acc * acc, axis=-1, keepdims=True)
        acc = acc * lax.rsqrt(ms + EPS) * gf_ref[...]
    o_ref[...] = acc


def _out_mlp(h, ypg, yd, wo, g2, w1, w2, gf, tm, ffc, final):
    t = h.shape[0]
    return pl.pallas_call(
        functools.partial(_out_mlp_kernel, ffc=ffc, final=final),
        grid=(t // tm,),
        in_specs=[
            pl.BlockSpec((tm, D_MODEL), lambda i: (i, 0)),
            pl.BlockSpec((tm, POOL_WIDTH + GLA_WIDTH), lambda i: (i, 0)),
            pl.BlockSpec((tm, DIFF_WIDTH), lambda i: (i, 0)),
            _resident((D_MODEL, D_MODEL)),
            _resident((1, D_MODEL)),
            _resident((D_MODEL, D_FF)),
            _resident((D_FF, D_MODEL)),
            _resident((1, D_MODEL)),
        ],
        out_specs=pl.BlockSpec((tm, D_MODEL), lambda i: (i, 0)),
        out_shape=jax.ShapeDtypeStruct((t, D_MODEL), F32),
        compiler_params=pltpu.CompilerParams(
            dimension_semantics=("parallel",), vmem_limit_bytes=VMEM_LIMIT),
        name="out_mlp",
    )(h, ypg, yd, wo, g2, w1, w2, gf)


def _prep_w_in(w):
    pool = w[:, 0:256]
    dq = w[:, 256:768] * (DIFF_QKDIM ** -0.5)
    dk = w[:, 768:1280]
    dv = w[:, 1280:1792]
    gq = w[:, 1792:1920]
    gk = w[:, 1920:2048]
    gv = w[:, 2048:2304]
    gr = w[:, 2304:2560]
    gg = w[:, 2560:2576]
    pad = jnp.zeros((D_MODEL, REST_W - REST_GG - GLA_GATE_RANK), w.dtype)
    return jnp.concatenate([dq, dk, dv, pool, gq, gk, gv, gr, gg, pad], axis=1).astype(BF16)


def _prep_w_out(w):
    return jnp.concatenate([w[0:256], w[768:1024], w[256:768]], axis=0).astype(BF16)


def _prep_pool_w(pw):
    out = jnp.zeros((POOL_WIDTH, POOL_WIDTH), F32)
    for g in range(len(POOL_WINDOWS)):
        out = out.at[g * POOL_GDIM:(g + 1) * POOL_GDIM, g * POOL_GDIM:(g + 1) * POOL_GDIM].set(pw[g])
    return out.astype(BF16)


def _tiles(seq, t):
    tm = math.gcd(t, 512)
    tq = math.gcd(seq, 256)
    ts = math.gcd(seq, 256)
    return tm, tq, ts


def kernel(x, norm1_g, w_in, pool_w, pool_scale, diff_lq1, diff_lk1, diff_lq2, diff_lk2, diff_norm_g, gla_w_gate2, gla_b_gate, gla_norm_g, w_out, norm2_g, w_mlp1, w_mlp2, final_norm_g):
    batch, seq, d = x.shape
    assert d == D_MODEL and seq % CHUNK == 0
    depth = w_in.shape[0]
    t = batch * seq
    tm, tq, ts = _tiles(seq, t)
    h = x.reshape(t, D_MODEL)
    gf = final_norm_g.reshape(1, D_MODEL)
    for l in range(depth):
        lambda_init = 0.8 - 0.6 * math.exp(-0.3 * l)
        qkv, rest = _inproj(h, norm1_g[l].reshape(1, D_MODEL), _prep_w_in(w_in[l]), tm)
        lam = _diff_lambda(diff_lq1[l], diff_lk1[l], diff_lq2[l], diff_lk2[l], lambda_init)
        y_diff = _diff_attention(qkv, lam, diff_norm_g[l].reshape(1, DIFF_WIDTH),
                                 batch, seq, tq, lambda_init)
        wg2 = jnp.zeros((LANES, GLA_KW), F32).at[0:GLA_GATE_RANK].set(gla_w_gate2[l])
        y_pg = _gla_pool(rest, wg2, gla_b_gate[l].reshape(1, GLA_KW),
                         gla_norm_g[l].reshape(1, GLA_WIDTH), _prep_pool_w(pool_w[l]),
                         pool_scale[l].reshape(1, POOL_WIDTH), batch, seq, ts)
        h = _out_mlp(h, y_pg, y_diff, _prep_w_out(w_out[l]), norm2_g[l].reshape(1, D_MODEL),
                     w_mlp1[l].astype(BF16), w_mlp2[l].astype(BF16), gf, tm, 512,
                     final=(l == depth - 1))
    return h.reshape(batch, seq, D_MODEL)
```

```python
import functools
import math

import jax
import jax.numpy as jnp
from jax import lax
from jax.experimental import pallas as pl
from jax.experimental.pallas import tpu as pltpu

F32 = jnp.float32
BF16 = jnp.bfloat16

D_MODEL = 1024
CHUNK = 64
POOL_WIDTH = 256
POOL_WINDOWS = (2, 4, 8, 16)
POOL_GDIM = 64
POOL_HALO = 16
DIFF_HEADS = 4
DIFF_VDIM = 128
DIFF_QKDIM = 64
DIFF_WIDTH = 512
GLA_HEADS = 4
GLA_VDIM = 64
GLA_KDIM = 32
GLA_WIDTH = 256
GLA_KW = GLA_HEADS * GLA_KDIM
GLA_GATE_RANK = 16
GLA_GATE_TAU = 16.0
D_FF = 4096
EPS = 1e-6

LANES = 128
QKV_W = 3 * DIFF_WIDTH
REST_POOL = 0
REST_GQ = 256
REST_GK = 384
REST_GV = 512
REST_GR = 768
REST_GG = 1024
REST_W = 1152
W_IN_COLS = QKV_W + REST_W

VMEM_LIMIT = 56 * 1024 * 1024
NEG_BIG = -1e30
LOG2_E = 1.4426950408889634
ATTN_ROW_BLOCK = 256


def _resident(shape):
    nd = len(shape)
    return pl.BlockSpec(shape, lambda *_: (0,) * nd, pipeline_mode=pl.Buffered(1))


def _split_bf16(a):
    hi = a.astype(BF16)
    lo = (a - hi.astype(F32)).astype(BF16)
    return hi, lo


def _inproj_kernel(h_ref, g_ref, w_ref, qkv_ref, rest_ref):
    h = h_ref[...]
    ms = jnp.mean(h * h, axis=-1, keepdims=True)
    u = (h * lax.rsqrt(ms + EPS) * g_ref[...]).astype(BF16)
    qkv_ref[...] = jnp.dot(u, w_ref[:, :QKV_W], preferred_element_type=F32).astype(BF16)
    rest_ref[...] = jnp.dot(u, w_ref[:, QKV_W:], preferred_element_type=F32)


def _inproj(h, g, w, tm):
    t = h.shape[0]
    return pl.pallas_call(
        _inproj_kernel,
        grid=(t // tm,),
        in_specs=[
            pl.BlockSpec((tm, D_MODEL), lambda i: (i, 0)),
            _resident((1, D_MODEL)),
            _resident((D_MODEL, W_IN_COLS)),
        ],
        out_specs=[
            pl.BlockSpec((tm, QKV_W), lambda i: (i, 0)),
            pl.BlockSpec((tm, REST_W), lambda i: (i, 0)),
        ],
        out_shape=[
            jax.ShapeDtypeStruct((t, QKV_W), BF16),
            jax.ShapeDtypeStruct((t, REST_W), F32),
        ],
        compiler_params=pltpu.CompilerParams(
            dimension_semantics=("parallel",), vmem_limit_bytes=VMEM_LIMIT),
        name="inproj",
    )(h, g, w)


def _attn_kernel(q_ref, k_ref, v_ref, lam_ref, g_ref, o_ref,
                 qs_scr, va_scr, sa_scr, sb_scr, m_scr, acc_scr, *, tq, rb, lambda_init):
    qi = pl.program_id(2)
    m_rows = 2 * tq
    nblk = m_rows // rb
    reps = tq // LANES

    @pl.when(qi == 0)
    def _():
        va_scr[:, 0:DIFF_VDIM] = v_ref[...]
        va_scr[:, DIFF_VDIM:2 * DIFF_VDIM] = jnp.ones((va_scr.shape[0], DIFF_VDIM), BF16)

    q = q_ref[...]
    lane = lax.broadcasted_iota(jnp.int32, (tq, LANES), 1)
    zero = jnp.zeros_like(q)
    qs_scr[0:tq, :] = jnp.where(lane < DIFF_QKDIM, q, zero)
    qs_scr[tq:m_rows, :] = jnp.where(lane >= DIFF_QKDIM, q, zero)
    m_scr[...] = jnp.full(m_scr.shape, -jnp.inf, F32)
    acc_scr[...] = jnp.zeros(acc_scr.shape, F32)

    def qk_block(j, s_dst, r):
        k = k_ref[pl.ds(pl.multiple_of(j * tq, tq), tq), :]
        s_dst[r * rb:(r + 1) * rb, :] = lax.dot_general(
            qs_scr[r * rb:(r + 1) * rb, :], k, (((1,), (1,)), ((), ())),
            preferred_element_type=F32)

    def softmax_pv_block(j, s_src, r, masked):
        rows = slice(r * rb, (r + 1) * rb)
        s = s_src[rows, :]
        if masked:
            row = lax.broadcasted_iota(jnp.int32, (rb, tq), 0) + (r * rb) % tq
            col = lax.broadcasted_iota(jnp.int32, (rb, tq), 1)
            s = jnp.where(col // CHUNK <= row // CHUNK, s, NEG_BIG)
        m_prev = m_scr[rows, :]
        m_next = jnp.maximum(m_prev, jnp.max(s, axis=1, keepdims=True))
        alpha = jnp.exp2(m_prev - m_next)
        p = jnp.exp2(s - jnp.concatenate([m_next] * reps, axis=1))
        m_scr[rows, :] = m_next
        va = va_scr[pl.ds(pl.multiple_of(j * tq, tq), tq), :]
        acc_scr[rows, :] = (jnp.concatenate([alpha, alpha], axis=1) * acc_scr[rows, :]
                            + jnp.dot(p.astype(BF16), va, preferred_element_type=F32))

    def stage(j_next, s_next, j_cur, s_cur, masked=False):
        for r in range(nblk):
            if j_next is not None:
                qk_block(j_next, s_next, r)
            softmax_pv_block(j_cur, s_cur, r, masked)

    for r in range(nblk):
        qk_block(0, sa_scr, r)

    def body(i, carry):
        stage(2 * i + 1, sb_scr, 2 * i, sa_scr)
        stage(2 * i + 2, sa_scr, 2 * i + 1, sb_scr)
        return carry

    lax.fori_loop(0, qi // 2, body, 0)

    @pl.when(qi % 2 == 0)
    def _():
        stage(None, None, qi, sa_scr, masked=True)

    @pl.when(qi % 2 == 1)
    def _():
        stage(qi, sb_scr, qi - 1, sa_scr)
        stage(None, None, qi, sb_scr, masked=True)

    acc = acc_scr[...]
    out = acc[:, 0:DIFF_VDIM] / acc[:, DIFF_VDIM:2 * DIFF_VDIM]
    o = out[0:tq] - lam_ref[...] * out[tq:m_rows]
    ms = jnp.mean(o * o, axis=-1, keepdims=True)
    o = o * lax.rsqrt(ms + EPS) * g_ref[...] * (1.0 - lambda_init)
    o_ref[...] = o.astype(o_ref.dtype)


def _lambda_kernel(lq1_ref, lk1_ref, lq2_ref, lk2_ref, lam_ref, *, lambda_init):
    a = jnp.sum(lq1_ref[...] * lk1_ref[...], axis=-1, keepdims=True)
    b = jnp.sum(lq2_ref[...] * lk2_ref[...], axis=-1, keepdims=True)
    lam_ref[...] = jnp.broadcast_to(jnp.exp(a) - jnp.exp(b) + lambda_init, lam_ref.shape)


def _diff_lambda(lq1, lk1, lq2, lk2, lambda_init):
    args = [a.reshape(1, DIFF_QKDIM) for a in (lq1, lk1, lq2, lk2)]
    return pl.pallas_call(
        functools.partial(_lambda_kernel, lambda_init=lambda_init),
        out_shape=jax.ShapeDtypeStruct((1, LANES), F32),
        name="diff_lambda",
    )(*args)


def _diff_attention(qkv, lam, norm_g, batch, seq, tq, lambda_init):
    t = batch * seq
    nq = seq // tq
    kcol = DIFF_WIDTH // DIFF_VDIM
    vcol = 2 * DIFF_WIDTH // DIFF_VDIM
    return pl.pallas_call(
        functools.partial(_attn_kernel, tq=tq, rb=min(ATTN_ROW_BLOCK, 2 * tq),
                          lambda_init=lambda_init),
        grid=(batch, DIFF_HEADS, nq),
        in_specs=[
            pl.BlockSpec((tq, DIFF_VDIM), lambda b, h, i: (b * nq + i, h)),
            pl.BlockSpec((seq, DIFF_VDIM), lambda b, h, i: (b, kcol + h)),
            pl.BlockSpec((seq, DIFF_VDIM), lambda b, h, i: (b, vcol + h)),
            pl.BlockSpec((1, LANES), lambda b, h, i: (0, 0)),
            pl.BlockSpec((1, DIFF_VDIM), lambda b, h, i: (0, h)),
        ],
        out_specs=pl.BlockSpec((tq, DIFF_VDIM), lambda b, h, i: (b * nq + i, h)),
        out_shape=jax.ShapeDtypeStruct((t, DIFF_WIDTH), BF16),
        scratch_shapes=[
            pltpu.VMEM((2 * tq, LANES), BF16),
            pltpu.VMEM((seq, 2 * DIFF_VDIM), BF16),
            pltpu.VMEM((2 * tq, tq), F32),
            pltpu.VMEM((2 * tq, tq), F32),
            pltpu.VMEM((2 * tq, LANES), F32),
            pltpu.VMEM((2 * tq, 2 * DIFF_VDIM), F32),
        ],
        compiler_params=pltpu.CompilerParams(
            dimension_semantics=("arbitrary", "arbitrary", "arbitrary"),
            vmem_limit_bytes=VMEM_LIMIT),
        name="diff_attention",
    )(qkv, qkv, qkv, lam, norm_g)


def _gla_pool_kernel(rest_ref, wg2_ref, bg_ref, gng_ref, pw_ref, ps_ref, out_ref,
                     state_scr, halo_scr, *, ts):
    st = pl.program_id(1)
    nchunk = ts // CHUNK

    @pl.when(st == 0)
    def _():
        state_scr[...] = jnp.zeros_like(state_scr)
        halo_scr[0:POOL_HALO, :] = jnp.zeros((POOL_HALO, POOL_WIDTH), F32)

    x = rest_ref[:, REST_POOL:REST_POOL + POOL_WIDTH]
    halo_scr[POOL_HALO:POOL_HALO + ts, :] = x

    def shifted(k, lo, hi):
        return halo_scr[POOL_HALO - k:POOL_HALO - k + ts, lo:hi]

    xa = x[:, 0:LANES]
    c2 = xa + shifted(1, 0, LANES)
    c4 = c2 + shifted(2, 0, LANES) + shifted(3, 0, LANES)
    xb = x[:, LANES:2 * LANES]
    c8 = xb
    for k in range(1, 8):
        c8 = c8 + shifted(k, LANES, 2 * LANES)
    c16 = c8
    for k in range(8, 16):
        c16 = c16 + shifted(k, LANES, 2 * LANES)
    halo_scr[0:POOL_HALO, :] = halo_scr[ts:ts + POOL_HALO, :]

    tpos = (st * ts + lax.broadcasted_iota(jnp.int32, (ts, LANES), 0) + 1).astype(F32)
    lane = lax.broadcasted_iota(jnp.int32, (ts, LANES), 1)
    first = lane < POOL_GDIM
    mean_a = jnp.where(first, c2 / jnp.minimum(tpos, 2.0), c4 / jnp.minimum(tpos, 4.0))
    mean_b = jnp.where(first, c8 / jnp.minimum(tpos, 8.0), c16 / jnp.minimum(tpos, 16.0))
    pooled = jnp.concatenate([mean_a - xa, mean_b - xb], axis=1).astype(BF16)
    y_pool = jnp.dot(pooled, pw_ref[...], preferred_element_type=F32) * ps_ref[...]
    out_ref[:, 0:POOL_WIDTH] = y_pool.astype(out_ref.dtype)

    gq = rest_ref[:, REST_GQ:REST_GQ + GLA_KW]
    gk = rest_ref[:, REST_GK:REST_GK + GLA_KW]
    gv = rest_ref[:, REST_GV:REST_GV + GLA_WIDTH]
    gr = rest_ref[:, REST_GR:REST_GR + GLA_WIDTH]
    gg = rest_ref[:, REST_GG:REST_GG + LANES]

    g_hi, g_lo = _split_bf16(gg)
    w_hi, w_lo = _split_bf16(wg2_ref[...])
    xg = (jnp.dot(g_hi, w_hi, preferred_element_type=F32)
          + jnp.dot(g_lo, w_hi, preferred_element_type=F32)
          + jnp.dot(g_hi, w_lo, preferred_element_type=F32)) + bg_ref[...]
    log_a = (jnp.minimum(xg, 0.0) - jnp.log1p(jnp.exp(-jnp.abs(xg)))) * (1.0 / GLA_GATE_TAU)

    r = lax.broadcasted_iota(jnp.int32, (ts, ts), 0)
    c = lax.broadcasted_iota(jnp.int32, (ts, ts), 1)
    same_chunk = (r // CHUNK) == (c // CHUNK)
    causal = jnp.logical_and(same_chunk, c <= r)
    tri = jnp.where(causal, 1.0, 0.0).astype(BF16)
    blk = jnp.where(same_chunk, 1.0, 0.0).astype(BF16)
    la_hi, la_lo = _split_bf16(log_a)
    bcum = (jnp.dot(tri, la_hi, preferred_element_type=F32)
            + jnp.dot(tri, la_lo, preferred_element_type=F32))
    btot = (jnp.dot(blk, la_hi, preferred_element_type=F32)
            + jnp.dot(blk, la_lo, preferred_element_type=F32))

    q_dec = (gq * (GLA_KDIM ** -0.5)) * jnp.exp(bcum)
    k_inv = (gk * jnp.exp(-bcum)).astype(BF16)
    k_end = (gk * jnp.exp(btot - bcum)).astype(BF16)
    decay = jnp.exp(btot)
    gv_b = gv.astype(BF16)

    klane = lax.broadcasted_iota(jnp.int32, (ts, GLA_KW), 1)
    vlane = lax.broadcasted_iota(jnp.int32, (ts, GLA_WIDTH), 1)
    q_dec_b = q_dec.astype(BF16)
    o = jnp.zeros((ts, GLA_WIDTH), F32)
    for h in range(GLA_HEADS):
        qh = jnp.where(klane // GLA_KDIM == h, q_dec_b, jnp.zeros_like(q_dec_b))
        att = lax.dot_general(qh, k_inv, (((1,), (1,)), ((), ())), preferred_element_type=F32)
        att = jnp.where(causal, att, 0.0).astype(BF16)
        vh = jnp.where(vlane // GLA_VDIM == h, gv_b, jnp.zeros_like(gv_b))
        o = o + jnp.dot(att, vh, preferred_element_type=F32)

    srow = lax.broadcasted_iota(jnp.int32, (GLA_WIDTH, GLA_KW), 0)
    scol = lax.broadcasted_iota(jnp.int32, (GLA_WIDTH, GLA_KW), 1)
    head_diag = (srow // GLA_VDIM) == (scol // GLA_KDIM)
    state = state_scr[...]
    o_inter = []
    for n in range(nchunk):
        sl = slice(n * CHUNK, (n + 1) * CHUNK)
        o_inter.append(lax.dot_general(q_dec_b[sl], state.astype(BF16), (((1,), (1,)), ((), ())),
                                       preferred_element_type=F32))
        kv_t = lax.dot_general(gv_b[sl], k_end[sl], (((0,), (0,)), ((), ())),
                               preferred_element_type=F32)
        state = state * decay[n * CHUNK:n * CHUNK + 1, :] + jnp.where(head_diag, kv_t, 0.0)
    state_scr[...] = state
    o = o + jnp.concatenate(o_inter, axis=0)

    hr = lax.broadcasted_iota(jnp.int32, (GLA_WIDTH, GLA_WIDTH), 0)
    hc = lax.broadcasted_iota(jnp.int32, (GLA_WIDTH, GLA_WIDTH), 1)
    ones_blk = jnp.where((hr // GLA_VDIM) == (hc // GLA_VDIM), 1.0, 0.0).astype(BF16)
    sq_hi, sq_lo = _split_bf16(o * o)
    ms = (jnp.dot(sq_hi, ones_blk, preferred_element_type=F32)
          + jnp.dot(sq_lo, ones_blk, preferred_element_type=F32)) * (1.0 / GLA_VDIM)
    o = o * lax.rsqrt(ms + EPS) * gng_ref[...]
    silu = gr / (1.0 + jnp.exp(-gr))
    out_ref[:, POOL_WIDTH:POOL_WIDTH + GLA_WIDTH] = (o * silu).astype(out_ref.dtype)


def _gla_pool(rest, wg2, bg, gng, pw, ps, batch, seq, ts):
    t = batch * seq
    ns = seq // ts
    return pl.pallas_call(
        functools.partial(_gla_pool_kernel, ts=ts),
        grid=(batch, ns),
        in_specs=[
            pl.BlockSpec((ts, REST_W), lambda b, s: (b * ns + s, 0)),
            _resident((LANES, GLA_KW)),
            _resident((1, GLA_KW)),
            _resident((1, GLA_WIDTH)),
            _resident((POOL_WIDTH, POOL_WIDTH)),
            _resident((1, POOL_WIDTH)),
        ],
        out_specs=pl.BlockSpec((ts, POOL_WIDTH + GLA_WIDTH), lambda b, s: (b * ns + s, 0)),
        out_shape=jax.ShapeDtypeStruct((t, POOL_WIDTH + GLA_WIDTH), BF16),
        scratch_shapes=[
            pltpu.VMEM((GLA_WIDTH, GLA_KW), F32),
            pltpu.VMEM((POOL_HALO + ts, POOL_WIDTH), F32),
        ],
        compiler_params=pltpu.CompilerParams(
            dimension_semantics=("parallel", "arbitrary"), vmem_limit_bytes=VMEM_LIMIT),
        name="gla_pool",
    )(rest, wg2, bg, gng, pw, ps)


def _out_mlp_kernel(h_ref, ypg_ref, yd_ref, wo_ref, g2_ref, w1_ref, w2_ref, gf_ref, o_ref,
                    *, ffc, final):
    npg = POOL_WIDTH + GLA_WIDTH
    h1 = (h_ref[...]
          + jnp.dot(ypg_ref[...], wo_ref[0:npg, :], preferred_element_type=F32)
          + jnp.dot(yd_ref[...], wo_ref[npg:D_MODEL, :], preferred_element_type=F32))
    ms = jnp.mean(h1 * h1, axis=-1, keepdims=True)
    z = (h1 * lax.rsqrt(ms + EPS) * g2_ref[...]).astype(BF16)
    del ffc
    a = jnp.maximum(jnp.dot(z, w1_ref[...], preferred_element_type=F32), 0.0)
    acc = h1 + jnp.dot((a * a).astype(BF16), w2_ref[...], preferred_element_type=F32)
    if final:
        ms = jnp.mean(acc * acc, axis=-1, keepdims=True)
        acc = acc * lax.rsqrt(ms + EPS) * gf_ref[...]
    o_ref[...] = acc


def _out_mlp(h, ypg, yd, wo, g2, w1, w2, gf, tm, ffc, final):
    t = h.shape[0]
    return pl.pallas_call(
        functools.partial(_out_mlp_kernel, ffc=ffc, final=final),
        grid=(t // tm,),
        in_specs=[
            pl.BlockSpec((tm, D_MODEL), lambda i: (i, 0)),
            pl.BlockSpec((tm, POOL_WIDTH + GLA_WIDTH), lambda i: (i, 0)),
            pl.BlockSpec((tm, DIFF_WIDTH), lambda i: (i, 0)),
            _resident((D_MODEL, D_MODEL)),
            _resident((1, D_MODEL)),
            _resident((D_MODEL, D_FF)),
            _resident((D_FF, D_MODEL)),
            _resident((1, D_MODEL)),
        ],
        out_specs=pl.BlockSpec((tm, D_MODEL), lambda i: (i, 0)),
        out_shape=jax.ShapeDtypeStruct((t, D_MODEL), F32),
        compiler_params=pltpu.CompilerParams(
            dimension_semantics=("parallel",), vmem_limit_bytes=VMEM_LIMIT),
        name="out_mlp",
    )(h, ypg, yd, wo, g2, w1, w2, gf)


def _prep_w_in(w):
    pool = w[:, 0:256]
    dq = w[:, 256:768] * (DIFF_QKDIM ** -0.5 * LOG2_E)
    dk = w[:, 768:1280]
    dv = w[:, 1280:1792]
    gq = w[:, 1792:1920]
    gk = w[:, 1920:2048]
    gv = w[:, 2048:2304]
    gr = w[:, 2304:2560]
    gg = w[:, 2560:2576]
    pad = jnp.zeros((D_MODEL, REST_W - REST_GG - GLA_GATE_RANK), w.dtype)
    return jnp.concatenate([dq, dk, dv, pool, gq, gk, gv, gr, gg, pad], axis=1).astype(BF16)


def _prep_w_out(w):
    return jnp.concatenate([w[0:256], w[768:1024], w[256:768]], axis=0).astype(BF16)


def _prep_pool_w(pw):
    out = jnp.zeros((POOL_WIDTH, POOL_WIDTH), F32)
    for g in range(len(POOL_WINDOWS)):
        out = out.at[g * POOL_GDIM:(g + 1) * POOL_GDIM, g * POOL_GDIM:(g + 1) * POOL_GDIM].set(pw[g])
    return out.astype(BF16)


def _tiles(seq, t):
    tm = math.gcd(t, 512)
    tq = math.gcd(seq, 512)
    ts = math.gcd(seq, 256)
    return tm, tq, ts


def kernel(x, norm1_g, w_in, pool_w, pool_scale, diff_lq1, diff_lk1, diff_lq2, diff_lk2, diff_norm_g, gla_w_gate2, gla_b_gate, gla_norm_g, w_out, norm2_g, w_mlp1, w_mlp2, final_norm_g):
    batch, seq, d = x.shape
    assert d == D_MODEL and seq % CHUNK == 0
    depth = w_in.shape[0]
    t = batch * seq
    tm, tq, ts = _tiles(seq, t)
    h = x.reshape(t, D_MODEL)
    gf = final_norm_g.reshape(1, D_MODEL)
    for l in range(depth):
        lambda_init = 0.8 - 0.6 * math.exp(-0.3 * l)
        qkv, rest = _inproj(h, norm1_g[l].reshape(1, D_MODEL), _prep_w_in(w_in[l]), tm)
        lam = _diff_lambda(diff_lq1[l], diff_lk1[l], diff_lq2[l], diff_lk2[l], lambda_init)
        y_diff = _diff_attention(qkv, lam, diff_norm_g[l].reshape(1, DIFF_WIDTH),
                                 batch, seq, tq, lambda_init)
        wg2 = jnp.zeros((LANES, GLA_KW), F32).at[0:GLA_GATE_RANK].set(gla_w_gate2[l])
        y_pg = _gla_pool(rest, wg2, gla_b_gate[l].reshape(1, GLA_KW),
                         gla_norm_g[l].reshape(1, GLA_WIDTH), _prep_pool_w(pool_w[l]),
                         pool_scale[l].reshape(1, POOL_WIDTH), batch, seq, ts)
        h = _out_mlp(h, y_pg, y_diff, _prep_w_out(w_out[l]), norm2_g[l].reshape(1, D_MODEL),
                     w_mlp1[l].astype(BF16), w_mlp2[l].astype(BF16), gf, tm, 512,
                     final=(l == depth - 1))
    return h.reshape(batch, seq, D_MODEL)
```

```python
import functools
import math

import jax
import jax.numpy as jnp
from jax import lax
from jax.experimental import pallas as pl
from jax.experimental.pallas import tpu as pltpu

F32 = jnp.float32
BF16 = jnp.bfloat16

D_MODEL = 1024
CHUNK = 64
POOL_WIDTH = 256
POOL_WINDOWS = (2, 4, 8, 16)
POOL_GDIM = 64
POOL_HALO = 16
DIFF_HEADS = 4
DIFF_VDIM = 128
DIFF_QKDIM = 64
DIFF_WIDTH = 512
GLA_HEADS = 4
GLA_VDIM = 64
GLA_KDIM = 32
GLA_WIDTH = 256
GLA_KW = GLA_HEADS * GLA_KDIM
GLA_GATE_RANK = 16
GLA_GATE_TAU = 16.0
D_FF = 4096
EPS = 1e-6

LANES = 128
QKV_W = 3 * DIFF_WIDTH
REST_POOL = 0
REST_GQ = 256
REST_GK = 384
REST_GV = 512
REST_GR = 768
REST_GG = 1024
REST_W = 1152
W_IN_COLS = QKV_W + REST_W

VMEM_LIMIT = 56 * 1024 * 1024
NEG_BIG = -1e30
LOG2_E = 1.4426950408889634
ATTN_ROW_BLOCK = 256
ATTN_PAIRS_PER_TRIP = 2
ATTN_QK_ROW_BLOCK = 256


def _resident(shape):
    nd = len(shape)
    return pl.BlockSpec(shape, lambda *_: (0,) * nd, pipeline_mode=pl.Buffered(1))


def _split_bf16(a):
    hi = a.astype(BF16)
    lo = (a - hi.astype(F32)).astype(BF16)
    return hi, lo


def _inproj_kernel(h_ref, g_ref, w_ref, qkv_ref, rest_ref):
    h = h_ref[...]
    ms = jnp.mean(h * h, axis=-1, keepdims=True)
    u = (h * lax.rsqrt(ms + EPS) * g_ref[...]).astype(BF16)
    qkv_ref[...] = jnp.dot(u, w_ref[:, :QKV_W], preferred_element_type=F32).astype(BF16)
    rest_ref[...] = jnp.dot(u, w_ref[:, QKV_W:], preferred_element_type=F32)


def _inproj(h, g, w, tm):
    t = h.shape[0]
    return pl.pallas_call(
        _inproj_kernel,
        grid=(t // tm,),
        in_specs=[
            pl.BlockSpec((tm, D_MODEL), lambda i: (i, 0)),
            _resident((1, D_MODEL)),
            _resident((D_MODEL, W_IN_COLS)),
        ],
        out_specs=[
            pl.BlockSpec((tm, QKV_W), lambda i: (i, 0)),
            pl.BlockSpec((tm, REST_W), lambda i: (i, 0)),
        ],
        out_shape=[
            jax.ShapeDtypeStruct((t, QKV_W), BF16),
            jax.ShapeDtypeStruct((t, REST_W), F32),
        ],
        compiler_params=pltpu.CompilerParams(
            dimension_semantics=("parallel",), vmem_limit_bytes=VMEM_LIMIT),
        name="inproj",
    )(h, g, w)


def _attn_kernel(q_ref, k_ref, v_ref, lam_ref, g_ref, o_ref,
                 qs_scr, e_scr, ka_scr, va_scr, sa_scr, sb_scr, la_scr, lb_scr, m_scr, acc_scr,
                 *, tq, rb, qrb, nq, lambda_init):
    m_rows = 2 * tq
    nblk = m_rows // rb
    reps = tq // LANES
    nitems = nq * (nq + 1) // 2

    va_scr[:, 0:DIFF_VDIM] = v_ref[...]
    va_scr[:, DIFF_VDIM:2 * DIFF_VDIM] = jnp.ones((va_scr.shape[0], DIFF_VDIM), BF16)

    lane = lax.broadcasted_iota(jnp.int32, (tq, LANES), 1)

    def stack_q(i, carry):
        q = q_ref[pl.ds(pl.multiple_of(i * tq, tq), tq), :]
        zero = jnp.zeros_like(q)
        qs_scr[i, 0:tq, :] = jnp.where(lane < DIFF_QKDIM, q, zero)
        qs_scr[i, tq:m_rows, :] = jnp.where(lane >= DIFF_QKDIM, q, zero)
        return carry

    lax.fori_loop(0, nq, stack_q, 0)

    seq = k_ref.shape[0]
    ka_scr[:, 0:LANES] = k_ref[...]
    kchunk = (lax.broadcasted_iota(jnp.int32, (seq, LANES), 0) % tq) // CHUNK
    klane = lax.broadcasted_iota(jnp.int32, (seq, LANES), 1)
    ka_scr[:, LANES:2 * LANES] = jnp.where(kchunk > klane, 1.0, 0.0).astype(BF16)
    qchunk = (lax.broadcasted_iota(jnp.int32, (m_rows, LANES), 0) % tq) // CHUNK
    qlane = lax.broadcasted_iota(jnp.int32, (m_rows, LANES), 1)
    e_scr[0] = jnp.zeros((m_rows, LANES), BF16)
    e_scr[1] = jnp.where(qchunk == qlane, NEG_BIG, 0.0).astype(BF16)
    acc_scr[...] = jnp.zeros(acc_scr.shape, F32)

    def qk_block(item, s_dst, r):
        qi, j = item
        rows = slice(r * qrb, (r + 1) * qrb)
        k = ka_scr[pl.ds(pl.multiple_of(j * tq, tq), tq), :]
        q = jnp.concatenate([qs_scr[qi, rows, :], e_scr[(j == qi).astype(jnp.int32), rows, :]],
                            axis=1)
        s = lax.dot_general(q, k, (((1,), (1,)), ((), ())), preferred_element_type=F32)
        s_dst[0][rows, :] = s
        lane_max = s[:, 0:LANES]
        for c in range(1, reps):
            lane_max = jnp.maximum(lane_max, s[:, c * LANES:(c + 1) * LANES])
        s_dst[1][rows, :] = lane_max

    def softmax_pv_block(item, s_src, r):
        qi, j = item
        rows = slice(r * rb, (r + 1) * rb)
        m_prev = jnp.where(j == 0, -jnp.inf, m_scr[rows, :])
        m_next = jnp.maximum(m_prev, jnp.max(s_src[1][rows, :], axis=1, keepdims=True))
        alpha = jnp.exp2(m_prev - m_next)
        p = jnp.exp2(s_src[0][rows, :] - jnp.concatenate([m_next] * reps, axis=1))
        m_scr[rows, :] = m_next
        va = va_scr[pl.ds(pl.multiple_of(j * tq, tq), tq), :]
        acc_scr[qi, rows, :] = (jnp.concatenate([alpha, alpha], axis=1) * acc_scr[qi, rows, :]
                                + jnp.dot(p.astype(BF16), va, preferred_element_type=F32))

    def stage(item_next, s_next, item_cur, s_cur):
        per = qrb // rb
        for r in range(nblk):
            if item_next is not None and r % per == 0:
                qk_block(item_next, s_next, r // per)
            softmax_pv_block(item_cur, s_cur, r)

    def following(item):
        qi, j = item
        last = j == qi
        qn = jnp.where(last, qi + 1, qi)
        jn = jnp.where(last, 0, j + 1)
        over = qn >= nq
        return jnp.where(over, qi, qn), jnp.where(over, j, jn)

    first = (jnp.int32(0), jnp.int32(0))
    buf_a = (sa_scr, la_scr)
    buf_b = (sb_scr, lb_scr)
    for r in range(m_rows // qrb):
        qk_block(first, buf_a, r)

    def pair(item0):
        item1 = following(item0)
        item2 = following(item1)
        stage(item1, buf_b, item0, buf_a)
        stage(item2, buf_a, item1, buf_b)
        return item2

    def body(_, item):
        for _ in range(ATTN_PAIRS_PER_TRIP):
            item = pair(item)
        return item

    npairs = nitems // 2
    item = lax.fori_loop(0, npairs // ATTN_PAIRS_PER_TRIP, body, first)
    for _ in range(npairs % ATTN_PAIRS_PER_TRIP):
        item = pair(item)
    if nitems % 2:
        stage(None, None, item, buf_a)

    def finalize(i, carry):
        acc = acc_scr[i]
        out = acc[:, 0:DIFF_VDIM] / acc[:, DIFF_VDIM:2 * DIFF_VDIM]
        o = out[0:tq] - lam_ref[...] * out[tq:m_rows]
        ms = jnp.mean(o * o, axis=-1, keepdims=True)
        o = o * lax.rsqrt(ms + EPS) * g_ref[...] * (1.0 - lambda_init)
        o_ref[pl.ds(pl.multiple_of(i * tq, tq), tq), :] = o.astype(o_ref.dtype)
        return carry

    lax.fori_loop(0, nq, finalize, 0)


def _lambda_kernel(lq1_ref, lk1_ref, lq2_ref, lk2_ref, lam_ref, *, lambda_init):
    a = jnp.sum(lq1_ref[...] * lk1_ref[...], axis=-1, keepdims=True)
    b = jnp.sum(lq2_ref[...] * lk2_ref[...], axis=-1, keepdims=True)
    lam_ref[...] = jnp.broadcast_to(jnp.exp(a) - jnp.exp(b) + lambda_init, lam_ref.shape)


def _diff_lambda(lq1, lk1, lq2, lk2, lambda_init):
    args = [a.reshape(1, DIFF_QKDIM) for a in (lq1, lk1, lq2, lk2)]
    return pl.pallas_call(
        functools.partial(_lambda_kernel, lambda_init=lambda_init),
        out_shape=jax.ShapeDtypeStruct((1, LANES), F32),
        name="diff_lambda",
    )(*args)


def _diff_attention(qkv, lam, norm_g, batch, seq, tq, lambda_init):
    t = batch * seq
    nq = seq // tq
    kcol = DIFF_WIDTH // DIFF_VDIM
    vcol = 2 * DIFF_WIDTH // DIFF_VDIM
    return pl.pallas_call(
        functools.partial(_attn_kernel, tq=tq, rb=min(ATTN_ROW_BLOCK, 2 * tq),
                          qrb=min(ATTN_QK_ROW_BLOCK, 2 * tq), nq=nq,
                          lambda_init=lambda_init),
        grid=(batch, DIFF_HEADS),
        in_specs=[
            pl.BlockSpec((seq, DIFF_VDIM), lambda b, h: (b, h)),
            pl.BlockSpec((seq, DIFF_VDIM), lambda b, h: (b, kcol + h)),
            pl.BlockSpec((seq, DIFF_VDIM), lambda b, h: (b, vcol + h)),
            pl.BlockSpec((1, LANES), lambda b, h: (0, 0)),
            pl.BlockSpec((1, DIFF_VDIM), lambda b, h: (0, h)),
        ],
        out_specs=pl.BlockSpec((seq, DIFF_VDIM), lambda b, h: (b, h)),
        out_shape=jax.ShapeDtypeStruct((t, DIFF_WIDTH), BF16),
        scratch_shapes=[
            pltpu.VMEM((nq, 2 * tq, LANES), BF16),
            pltpu.VMEM((2, 2 * tq, LANES), BF16),
            pltpu.VMEM((seq, 2 * LANES), BF16),
            pltpu.VMEM((seq, 2 * DIFF_VDIM), BF16),
            pltpu.VMEM((2 * tq, tq), F32),
            pltpu.VMEM((2 * tq, tq), F32),
            pltpu.VMEM((2 * tq, LANES), F32),
            pltpu.VMEM((2 * tq, LANES), F32),
            pltpu.VMEM((2 * tq, LANES), F32),
            pltpu.VMEM((nq, 2 * tq, 2 * DIFF_VDIM), F32),
        ],
        compiler_params=pltpu.CompilerParams(
            dimension_semantics=("parallel", "parallel"), vmem_limit_bytes=VMEM_LIMIT),
        name="diff_attention",
    )(qkv, qkv, qkv, lam, norm_g)


def _gla_pool_kernel(rest_ref, wg2_ref, bg_ref, gng_ref, pw_ref, ps_ref, out_ref,
                     state_scr, halo_scr, *, ts):
    st = pl.program_id(1)
    nchunk = ts // CHUNK

    @pl.when(st == 0)
    def _():
        state_scr[...] = jnp.zeros_like(state_scr)
        halo_scr[0:POOL_HALO, :] = jnp.zeros((POOL_HALO, POOL_WIDTH), F32)

    x = rest_ref[:, REST_POOL:REST_POOL + POOL_WIDTH]
    halo_scr[POOL_HALO:POOL_HALO + ts, :] = x

    def shifted(k, lo, hi):
        return halo_scr[POOL_HALO - k:POOL_HALO - k + ts, lo:hi]

    xa = x[:, 0:LANES]
    c2 = xa + shifted(1, 0, LANES)
    c4 = c2 + shifted(2, 0, LANES) + shifted(3, 0, LANES)
    xb = x[:, LANES:2 * LANES]
    c8 = xb
    for k in range(1, 8):
        c8 = c8 + shifted(k, LANES, 2 * LANES)
    c16 = c8
    for k in range(8, 16):
        c16 = c16 + shifted(k, LANES, 2 * LANES)
    halo_scr[0:POOL_HALO, :] = halo_scr[ts:ts + POOL_HALO, :]

    tpos = (st * ts + lax.broadcasted_iota(jnp.int32, (ts, LANES), 0) + 1).astype(F32)
    lane = lax.broadcasted_iota(jnp.int32, (ts, LANES), 1)
    first = lane < POOL_GDIM
    mean_a = jnp.where(first, c2 / jnp.minimum(tpos, 2.0), c4 / jnp.minimum(tpos, 4.0))
    mean_b = jnp.where(first, c8 / jnp.minimum(tpos, 8.0), c16 / jnp.minimum(tpos, 16.0))
    pooled = jnp.concatenate([mean_a - xa, mean_b - xb], axis=1).astype(BF16)
    y_pool = jnp.dot(pooled, pw_ref[...], preferred_element_type=F32) * ps_ref[...]
    out_ref[:, 0:POOL_WIDTH] = y_pool.astype(out_ref.dtype)

    gq = rest_ref[:, REST_GQ:REST_GQ + GLA_KW]
    gk = rest_ref[:, REST_GK:REST_GK + GLA_KW]
    gv = rest_ref[:, REST_GV:REST_GV + GLA_WIDTH]
    gr = rest_ref[:, REST_GR:REST_GR + GLA_WIDTH]
    gg = rest_ref[:, REST_GG:REST_GG + LANES]

    g_hi, g_lo = _split_bf16(gg)
    w_hi, w_lo = _split_bf16(wg2_ref[...])
    xg = (jnp.dot(g_hi, w_hi, preferred_element_type=F32)
          + jnp.dot(g_lo, w_hi, preferred_element_type=F32)
          + jnp.dot(g_hi, w_lo, preferred_element_type=F32)) + bg_ref[...]
    log_a = (jnp.minimum(xg, 0.0) - jnp.log1p(jnp.exp(-jnp.abs(xg)))) * (1.0 / GLA_GATE_TAU)

    r = lax.broadcasted_iota(jnp.int32, (ts, ts), 0)
    c = lax.broadcasted_iota(jnp.int32, (ts, ts), 1)
    same_chunk = (r // CHUNK) == (c // CHUNK)
    causal = jnp.logical_and(same_chunk, c <= r)
    tri = jnp.where(causal, 1.0, 0.0).astype(BF16)
    blk = jnp.where(same_chunk, 1.0, 0.0).astype(BF16)
    la_hi, la_lo = _split_bf16(log_a)
    bcum = (jnp.dot(tri, la_hi, preferred_element_type=F32)
            + jnp.dot(tri, la_lo, preferred_element_type=F32))
    btot = (jnp.dot(blk, la_hi, preferred_element_type=F32)
            + jnp.dot(blk, la_lo, preferred_element_type=F32))

    q_dec = (gq * (GLA_KDIM ** -0.5)) * jnp.exp(bcum)
    k_inv = (gk * jnp.exp(-bcum)).astype(BF16)
    k_end = (gk * jnp.exp(btot - bcum)).astype(BF16)
    decay = jnp.exp(btot)
    gv_b = gv.astype(BF16)

    klane = lax.broadcasted_iota(jnp.int32, (ts, GLA_KW), 1)
    vlane = lax.broadcasted_iota(jnp.int32, (ts, GLA_WIDTH), 1)
    q_dec_b = q_dec.astype(BF16)
    o = jnp.zeros((ts, GLA_WIDTH), F32)
    for h in range(GLA_HEADS):
        qh = jnp.where(klane // GLA_KDIM == h, q_dec_b, jnp.zeros_like(q_dec_b))
        att = lax.dot_general(qh, k_inv, (((1,), (1,)), ((), ())), preferred_element_type=F32)
        att = jnp.where(causal, att, 0.0).astype(BF16)
        vh = jnp.where(vlane // GLA_VDIM == h, gv_b, jnp.zeros_like(gv_b))
        o = o + jnp.dot(att, vh, preferred_element_type=F32)

    srow = lax.broadcasted_iota(jnp.int32, (GLA_WIDTH, GLA_KW), 0)
    scol = lax.broadcasted_iota(jnp.int32, (GLA_WIDTH, GLA_KW), 1)
    head_diag = (srow // GLA_VDIM) == (scol // GLA_KDIM)
    state = state_scr[...]
    o_inter = []
    for n in range(nchunk):
        sl = slice(n * CHUNK, (n + 1) * CHUNK)
        o_inter.append(lax.dot_general(q_dec_b[sl], state.astype(BF16), (((1,), (1,)), ((), ())),
                                       preferred_element_type=F32))
        kv_t = lax.dot_general(gv_b[sl], k_end[sl], (((0,), (0,)), ((), ())),
                               preferred_element_type=F32)
        state = state * decay[n * CHUNK:n * CHUNK + 1, :] + jnp.where(head_diag, kv_t, 0.0)
    state_scr[...] = state
    o = o + jnp.concatenate(o_inter, axis=0)

    hr = lax.broadcasted_iota(jnp.int32, (GLA_WIDTH, GLA_WIDTH), 0)
    hc = lax.broadcasted_iota(jnp.int32, (GLA_WIDTH, GLA_WIDTH), 1)
    ones_blk = jnp.where((hr // GLA_VDIM) == (hc // GLA_VDIM), 1.0, 0.0).astype(BF16)
    sq_hi, sq_lo = _split_bf16(o * o)
    ms = (jnp.dot(sq_hi, ones_blk, preferred_element_type=F32)
          + jnp.dot(sq_lo, ones_blk, preferred_element_type=F32)) * (1.0 / GLA_VDIM)
    o = o * lax.rsqrt(ms + EPS) * gng_ref[...]
    silu = gr / (1.0 + jnp.exp(-gr))
    out_ref[:, POOL_WIDTH:POOL_WIDTH + GLA_WIDTH] = (o * silu).astype(out_ref.dtype)


def _gla_pool(rest, wg2, bg, gng, pw, ps, batch, seq, ts):
    t = batch * seq
    ns = seq // ts
    return pl.pallas_call(
        functools.partial(_gla_pool_kernel, ts=ts),
        grid=(batch, ns),
        in_specs=[
            pl.BlockSpec((ts, REST_W), lambda b, s: (b * ns + s, 0)),
            _resident((LANES, GLA_KW)),
            _resident((1, GLA_KW)),
            _resident((1, GLA_WIDTH)),
            _resident((POOL_WIDTH, POOL_WIDTH)),
            _resident((1, POOL_WIDTH)),
        ],
        out_specs=pl.BlockSpec((ts, POOL_WIDTH + GLA_WIDTH), lambda b, s: (b * ns + s, 0)),
        out_shape=jax.ShapeDtypeStruct((t, POOL_WIDTH + GLA_WIDTH), BF16),
        scratch_shapes=[
            pltpu.VMEM((GLA_WIDTH, GLA_KW), F32),
            pltpu.VMEM((POOL_HALO + ts, POOL_WIDTH), F32),
        ],
        compiler_params=pltpu.CompilerParams(
            dimension_semantics=("parallel", "arbitrary"), vmem_limit_bytes=VMEM_LIMIT),
        name="gla_pool",
    )(rest, wg2, bg, gng, pw, ps)


def _out_mlp_kernel(h_ref, ypg_ref, yd_ref, wo_ref, g2_ref, w1_ref, w2_ref, gf_ref, o_ref,
                    *, final):
    npg = POOL_WIDTH + GLA_WIDTH
    h1 = (h_ref[...]
          + jnp.dot(ypg_ref[...], wo_ref[0:npg, :], preferred_element_type=F32)
          + jnp.dot(yd_ref[...], wo_ref[npg:D_MODEL, :], preferred_element_type=F32))
    ms = jnp.mean(h1 * h1, axis=-1, keepdims=True)
    z = (h1 * lax.rsqrt(ms + EPS) * g2_ref[...]).astype(BF16)
    a = jnp.maximum(jnp.dot(z, w1_ref[...], preferred_element_type=F32), 0.0)
    acc = h1 + jnp.dot((a * a).astype(BF16), w2_ref[...], preferred_element_type=F32)
    if final:
        ms = jnp.mean(acc * acc, axis=-1, keepdims=True)
        acc = acc * lax.rsqrt(ms + EPS) * gf_ref[...]
    o_ref[...] = acc


def _out_mlp(h, ypg, yd, wo, g2, w1, w2, gf, tm, final):
    t = h.shape[0]
    return pl.pallas_call(
        functools.partial(_out_mlp_kernel, final=final),
        grid=(t // tm,),
        in_specs=[
            pl.BlockSpec((tm, D_MODEL), lambda i: (i, 0)),
            pl.BlockSpec((tm, POOL_WIDTH + GLA_WIDTH), lambda i: (i, 0)),
            pl.BlockSpec((tm, DIFF_WIDTH), lambda i: (i, 0)),
            _resident((D_MODEL, D_MODEL)),
            _resident((1, D_MODEL)),
            _resident((D_MODEL, D_FF)),
            _resident((D_FF, D_MODEL)),
            _resident((1, D_MODEL)),
        ],
        out_specs=pl.BlockSpec((tm, D_MODEL), lambda i: (i, 0)),
        out_shape=jax.ShapeDtypeStruct((t, D_MODEL), F32),
        compiler_params=pltpu.CompilerParams(
            dimension_semantics=("parallel",), vmem_limit_bytes=VMEM_LIMIT),
        name="out_mlp",
    )(h, ypg, yd, wo, g2, w1, w2, gf)


def _prep_w_in(w):
    pool = w[:, 0:256]
    dq = w[:, 256:768] * (DIFF_QKDIM ** -0.5 * LOG2_E)
    dk = w[:, 768:1280]
    dv = w[:, 1280:1792]
    gq = w[:, 1792:1920]
    gk = w[:, 1920:2048]
    gv = w[:, 2048:2304]
    gr = w[:, 2304:2560]
    gg = w[:, 2560:2576]
    pad = jnp.zeros((D_MODEL, REST_W - REST_GG - GLA_GATE_RANK), w.dtype)
    return jnp.concatenate([dq, dk, dv, pool, gq, gk, gv, gr, gg, pad], axis=1).astype(BF16)


def _prep_w_out(w):
    return jnp.concatenate([w[0:256], w[768:1024], w[256:768]], axis=0).astype(BF16)


def _prep_pool_w(pw):
    out = jnp.zeros((POOL_WIDTH, POOL_WIDTH), F32)
    for g in range(len(POOL_WINDOWS)):
        out = out.at[g * POOL_GDIM:(g + 1) * POOL_GDIM, g * POOL_GDIM:(g + 1) * POOL_GDIM].set(pw[g])
    return out.astype(BF16)


def _tiles(seq, t):
    tm = math.gcd(t, 512)
    tq = math.gcd(seq, 512)
    ts = math.gcd(seq, 256)
    return tm, tq, ts


def kernel(x, norm1_g, w_in, pool_w, pool_scale, diff_lq1, diff_lk1, diff_lq2, diff_lk2, diff_norm_g, gla_w_gate2, gla_b_gate, gla_norm_g, w_out, norm2_g, w_mlp1, w_mlp2, final_norm_g):
    batch, seq, d = x.shape
    assert d == D_MODEL and seq % CHUNK == 0
    depth = w_in.shape[0]
    t = batch * seq
    tm, tq, ts = _tiles(seq, t)
    h = x.reshape(t, D_MODEL)
    gf = final_norm_g.reshape(1, D_MODEL)
    for l in range(depth):
        lambda_init = 0.8 - 0.6 * math.exp(-0.3 * l)
        qkv, rest = _inproj(h, norm1_g[l].reshape(1, D_MODEL), _prep_w_in(w_in[l]), tm)
        lam = _diff_lambda(diff_lq1[l], diff_lk1[l], diff_lq2[l], diff_lk2[l], lambda_init)
        y_diff = _diff_attention(qkv, lam, diff_norm_g[l].reshape(1, DIFF_WIDTH),
                                 batch, seq, tq, lambda_init)
        wg2 = jnp.zeros((LANES, GLA_KW), F32).at[0:GLA_GATE_RANK].set(gla_w_gate2[l])
        y_pg = _gla_pool(rest, wg2, gla_b_gate[l].reshape(1, GLA_KW),
                         gla_norm_g[l].reshape(1, GLA_WIDTH), _prep_pool_w(pool_w[l]),
                         pool_scale[l].reshape(1, POOL_WIDTH), batch, seq, ts)
        h = _out_mlp(h, y_pg, y_diff, _prep_w_out(w_out[l]), norm2_g[l].reshape(1, D_MODEL),
                     w_mlp1[l].astype(BF16), w_mlp2[l].astype(BF16), gf, tm,
                     final=(l == depth - 1))
    return h.reshape(batch, seq, D_MODEL)
```

```python
import functools
import math

import jax
import jax.numpy as jnp
from jax import lax
from jax.experimental import pallas as pl
from jax.experimental.pallas import tpu as pltpu

F32 = jnp.float32
BF16 = jnp.bfloat16

D_MODEL = 1024
CHUNK = 64
POOL_WIDTH = 256
POOL_WINDOWS = (2, 4, 8, 16)
POOL_GDIM = 64
POOL_HALO = 16
DIFF_HEADS = 4
DIFF_VDIM = 128
DIFF_QKDIM = 64
DIFF_WIDTH = 512
GLA_HEADS = 4
GLA_VDIM = 64
GLA_KDIM = 32
GLA_WIDTH = 256
GLA_KW = GLA_HEADS * GLA_KDIM
GLA_GATE_RANK = 16
GLA_GATE_TAU = 16.0
D_FF = 4096
EPS = 1e-6

LANES = 128
QKV_W = 3 * DIFF_WIDTH
REST_POOL = 0
REST_GQ = 256
REST_GK = 384
REST_GV = 512
REST_GR = 768
REST_GG = 1024
REST_W = 1152
W_IN_COLS = QKV_W + REST_W

VMEM_LIMIT = 56 * 1024 * 1024
NEG_BIG = -1e30
LOG2_E = 1.4426950408889634
ATTN_ROW_BLOCK = 512
ATTN_PAIRS_PER_TRIP = 4
ATTN_QK_ROW_BLOCK = 512


def _resident(shape):
    nd = len(shape)
    return pl.BlockSpec(shape, lambda *_: (0,) * nd, pipeline_mode=pl.Buffered(1))


def _split_bf16(a):
    hi = a.astype(BF16)
    lo = (a - hi.astype(F32)).astype(BF16)
    return hi, lo


def _inproj_kernel(h_ref, g_ref, w_ref, qkv_ref, rest_ref):
    h = h_ref[...]
    ms = jnp.mean(h * h, axis=-1, keepdims=True)
    u = (h * lax.rsqrt(ms + EPS) * g_ref[...]).astype(BF16)
    qkv_ref[...] = jnp.dot(u, w_ref[:, :QKV_W], preferred_element_type=F32).astype(BF16)
    rest_ref[...] = jnp.dot(u, w_ref[:, QKV_W:], preferred_element_type=F32)


def _inproj(h, g, w, tm):
    t = h.shape[0]
    return pl.pallas_call(
        _inproj_kernel,
        grid=(t // tm,),
        in_specs=[
            pl.BlockSpec((tm, D_MODEL), lambda i: (i, 0)),
            _resident((1, D_MODEL)),
            _resident((D_MODEL, W_IN_COLS)),
        ],
        out_specs=[
            pl.BlockSpec((tm, QKV_W), lambda i: (i, 0)),
            pl.BlockSpec((tm, REST_W), lambda i: (i, 0)),
        ],
        out_shape=[
            jax.ShapeDtypeStruct((t, QKV_W), BF16),
            jax.ShapeDtypeStruct((t, REST_W), F32),
        ],
        compiler_params=pltpu.CompilerParams(
            dimension_semantics=("parallel",), vmem_limit_bytes=VMEM_LIMIT),
        name="inproj",
    )(h, g, w)


def _attn_kernel(q_ref, k_ref, v_ref, lam_ref, g_ref, o_ref,
                 qs_scr, e_scr, ka_scr, va_scr, sa_scr, sb_scr, la_scr, lb_scr, m_scr, acc_scr,
                 *, tq, rb, qrb, nq, lambda_init):
    m_rows = 2 * tq
    nblk = m_rows // rb
    reps = tq // LANES
    nitems = nq * (nq + 1) // 2

    va_scr[:, 0:DIFF_VDIM] = v_ref[...]
    va_scr[:, DIFF_VDIM:2 * DIFF_VDIM] = jnp.ones((va_scr.shape[0], DIFF_VDIM), BF16)

    lane = lax.broadcasted_iota(jnp.int32, (tq, LANES), 1)

    def stack_q(i, carry):
        q = q_ref[pl.ds(pl.multiple_of(i * tq, tq), tq), :]
        zero = jnp.zeros_like(q)
        qs_scr[i, 0:tq, :] = jnp.where(lane < DIFF_QKDIM, q, zero)
        qs_scr[i, tq:m_rows, :] = jnp.where(lane >= DIFF_QKDIM, q, zero)
        return carry

    lax.fori_loop(0, nq, stack_q, 0)

    seq = k_ref.shape[0]
    ka_scr[:, 0:LANES] = k_ref[...]
    kchunk = (lax.broadcasted_iota(jnp.int32, (seq, LANES), 0) % tq) // CHUNK
    klane = lax.broadcasted_iota(jnp.int32, (seq, LANES), 1)
    ka_scr[:, LANES:2 * LANES] = jnp.where(kchunk > klane, 1.0, 0.0).astype(BF16)
    qchunk = (lax.broadcasted_iota(jnp.int32, (m_rows, LANES), 0) % tq) // CHUNK
    qlane = lax.broadcasted_iota(jnp.int32, (m_rows, LANES), 1)
    e_scr[0] = jnp.zeros((m_rows, LANES), BF16)
    e_scr[1] = jnp.where(qchunk == qlane, NEG_BIG, 0.0).astype(BF16)
    acc_scr[...] = jnp.zeros(acc_scr.shape, F32)

    def qk_block(item, s_dst, r):
        qi, j = item
        rows = slice(r * qrb, (r + 1) * qrb)
        k = ka_scr[pl.ds(pl.multiple_of(j * tq, tq), tq), :]
        q = jnp.concatenate([qs_scr[qi, rows, :], e_scr[(j == qi).astype(jnp.int32), rows, :]],
                            axis=1)
        s = lax.dot_general(q, k, (((1,), (1,)), ((), ())), preferred_element_type=F32)
        s_dst[0][rows, :] = s
        lane_max = s[:, 0:LANES]
        for c in range(1, reps):
            lane_max = jnp.maximum(lane_max, s[:, c * LANES:(c + 1) * LANES])
        s_dst[1][rows, :] = lane_max

    def softmax_pv_block(item, s_src, r):
        qi, j = item
        rows = slice(r * rb, (r + 1) * rb)
        m_prev = jnp.where(j == 0, -jnp.inf, m_scr[rows, :])
        m_next = jnp.maximum(m_prev, jnp.max(s_src[1][rows, :], axis=1, keepdims=True))
        alpha = jnp.exp2(m_prev - m_next)
        p = jnp.exp2(s_src[0][rows, :] - jnp.concatenate([m_next] * reps, axis=1))
        m_scr[rows, :] = m_next
        va = va_scr[pl.ds(pl.multiple_of(j * tq, tq), tq), :]
        acc_scr[qi, rows, :] = (jnp.concatenate([alpha, alpha], axis=1) * acc_scr[qi, rows, :]
                                + jnp.dot(p.astype(BF16), va, preferred_element_type=F32))

    def stage(item_next, s_next, item_cur, s_cur):
        per = qrb // rb
        for r in range(nblk):
            if item_next is not None and r % per == 0:
                qk_block(item_next, s_next, r // per)
            softmax_pv_block(item_cur, s_cur, r)

    def following(item):
        qi, j = item
        last = j == qi
        qn = jnp.where(last, qi + 1, qi)
        jn = jnp.where(last, 0, j + 1)
        over = qn >= nq
        return jnp.where(over, qi, qn), jnp.where(over, j, jn)

    first = (jnp.int32(0), jnp.int32(0))
    buf_a = (sa_scr, la_scr)
    buf_b = (sb_scr, lb_scr)
    for r in range(m_rows // qrb):
        qk_block(first, buf_a, r)

    def pair(item0):
        item1 = following(item0)
        item2 = following(item1)
        stage(item1, buf_b, item0, buf_a)
        stage(item2, buf_a, item1, buf_b)
        return item2

    def body(_, item):
        for _ in range(ATTN_PAIRS_PER_TRIP):
            item = pair(item)
        return item

    npairs = nitems // 2
    item = lax.fori_loop(0, npairs // ATTN_PAIRS_PER_TRIP, body, first)
    for _ in range(npairs % ATTN_PAIRS_PER_TRIP):
        item = pair(item)
    if nitems % 2:
        stage(None, None, item, buf_a)

    def finalize(i, carry):
        acc = acc_scr[i]
        out = acc[:, 0:DIFF_VDIM] / acc[:, DIFF_VDIM:2 * DIFF_VDIM]
        o = out[0:tq] - lam_ref[...] * out[tq:m_rows]
        ms = jnp.mean(o * o, axis=-1, keepdims=True)
        o = o * lax.rsqrt(ms + EPS) * g_ref[...] * (1.0 - lambda_init)
        o_ref[pl.ds(pl.multiple_of(i * tq, tq), tq), :] = o.astype(o_ref.dtype)
        return carry

    lax.fori_loop(0, nq, finalize, 0)


def _lambda_kernel(lq1_ref, lk1_ref, lq2_ref, lk2_ref, lam_ref, *, lambda_init):
    a = jnp.sum(lq1_ref[...] * lk1_ref[...], axis=-1, keepdims=True)
    b = jnp.sum(lq2_ref[...] * lk2_ref[...], axis=-1, keepdims=True)
    lam_ref[...] = jnp.broadcast_to(jnp.exp(a) - jnp.exp(b) + lambda_init, lam_ref.shape)


def _diff_lambda(lq1, lk1, lq2, lk2, lambda_init):
    args = [a.reshape(1, DIFF_QKDIM) for a in (lq1, lk1, lq2, lk2)]
    return pl.pallas_call(
        functools.partial(_lambda_kernel, lambda_init=lambda_init),
        out_shape=jax.ShapeDtypeStruct((1, LANES), F32),
        name="diff_lambda",
    )(*args)


def _diff_attention(qkv, lam, norm_g, batch, seq, tq, lambda_init):
    t = batch * seq
    nq = seq // tq
    kcol = DIFF_WIDTH // DIFF_VDIM
    vcol = 2 * DIFF_WIDTH // DIFF_VDIM
    return pl.pallas_call(
        functools.partial(_attn_kernel, tq=tq, rb=min(ATTN_ROW_BLOCK, 2 * tq),
                          qrb=min(ATTN_QK_ROW_BLOCK, 2 * tq), nq=nq,
                          lambda_init=lambda_init),
        grid=(batch, DIFF_HEADS),
        in_specs=[
            pl.BlockSpec((seq, DIFF_VDIM), lambda b, h: (b, h)),
            pl.BlockSpec((seq, DIFF_VDIM), lambda b, h: (b, kcol + h)),
            pl.BlockSpec((seq, DIFF_VDIM), lambda b, h: (b, vcol + h)),
            pl.BlockSpec((1, LANES), lambda b, h: (0, 0)),
            pl.BlockSpec((1, DIFF_VDIM), lambda b, h: (0, h)),
        ],
        out_specs=pl.BlockSpec((seq, DIFF_VDIM), lambda b, h: (b, h)),
        out_shape=jax.ShapeDtypeStruct((t, DIFF_WIDTH), BF16),
        scratch_shapes=[
            pltpu.VMEM((nq, 2 * tq, LANES), BF16),
            pltpu.VMEM((2, 2 * tq, LANES), BF16),
            pltpu.VMEM((seq, 2 * LANES), BF16),
            pltpu.VMEM((seq, 2 * DIFF_VDIM), BF16),
            pltpu.VMEM((2 * tq, tq), F32),
            pltpu.VMEM((2 * tq, tq), F32),
            pltpu.VMEM((2 * tq, LANES), F32),
            pltpu.VMEM((2 * tq, LANES), F32),
            pltpu.VMEM((2 * tq, LANES), F32),
            pltpu.VMEM((nq, 2 * tq, 2 * DIFF_VDIM), F32),
        ],
        compiler_params=pltpu.CompilerParams(
            dimension_semantics=("parallel", "parallel"), vmem_limit_bytes=VMEM_LIMIT),
        name="diff_attention",
    )(qkv, qkv, qkv, lam, norm_g)


def _gla_pool_kernel(rest_ref, wg2_ref, bg_ref, gng_ref, pw_ref, ps_ref, out_ref,
                     state_scr, halo_scr, *, ts):
    st = pl.program_id(1)
    nchunk = ts // CHUNK

    @pl.when(st == 0)
    def _():
        state_scr[...] = jnp.zeros_like(state_scr)
        halo_scr[0:POOL_HALO, :] = jnp.zeros((POOL_HALO, POOL_WIDTH), F32)

    x = rest_ref[:, REST_POOL:REST_POOL + POOL_WIDTH]
    halo_scr[POOL_HALO:POOL_HALO + ts, :] = x

    def shifted(k, lo, hi):
        return halo_scr[POOL_HALO - k:POOL_HALO - k + ts, lo:hi]

    xa = x[:, 0:LANES]
    c2 = xa + shifted(1, 0, LANES)
    c4 = c2 + shifted(2, 0, LANES) + shifted(3, 0, LANES)
    xb = x[:, LANES:2 * LANES]
    c8 = xb
    for k in range(1, 8):
        c8 = c8 + shifted(k, LANES, 2 * LANES)
    c16 = c8
    for k in range(8, 16):
        c16 = c16 + shifted(k, LANES, 2 * LANES)
    halo_scr[0:POOL_HALO, :] = halo_scr[ts:ts + POOL_HALO, :]

    tpos = (st * ts + lax.broadcasted_iota(jnp.int32, (ts, LANES), 0) + 1).astype(F32)
    lane = lax.broadcasted_iota(jnp.int32, (ts, LANES), 1)
    first = lane < POOL_GDIM
    mean_a = jnp.where(first, c2 / jnp.minimum(tpos, 2.0), c4 / jnp.minimum(tpos, 4.0))
    mean_b = jnp.where(first, c8 / jnp.minimum(tpos, 8.0), c16 / jnp.minimum(tpos, 16.0))
    pooled = jnp.concatenate([mean_a - xa, mean_b - xb], axis=1).astype(BF16)
    y_pool = jnp.dot(pooled, pw_ref[...], preferred_element_type=F32) * ps_ref[...]
    out_ref[:, 0:POOL_WIDTH] = y_pool.astype(out_ref.dtype)

    gq = rest_ref[:, REST_GQ:REST_GQ + GLA_KW]
    gk = rest_ref[:, REST_GK:REST_GK + GLA_KW]
    gv = rest_ref[:, REST_GV:REST_GV + GLA_WIDTH]
    gr = rest_ref[:, REST_GR:REST_GR + GLA_WIDTH]
    gg = rest_ref[:, REST_GG:REST_GG + LANES]

    g_hi, g_lo = _split_bf16(gg)
    w_hi, w_lo = _split_bf16(wg2_ref[...])
    xg = (jnp.dot(g_hi, w_hi, preferred_element_type=F32)
          + jnp.dot(g_lo, w_hi, preferred_element_type=F32)
          + jnp.dot(g_hi, w_lo, preferred_element_type=F32)) + bg_ref[...]
    log_a = (jnp.minimum(xg, 0.0) - jnp.log1p(jnp.exp(-jnp.abs(xg)))) * (1.0 / GLA_GATE_TAU)

    r = lax.broadcasted_iota(jnp.int32, (ts, ts), 0)
    c = lax.broadcasted_iota(jnp.int32, (ts, ts), 1)
    same_chunk = (r // CHUNK) == (c // CHUNK)
    causal = jnp.logical_and(same_chunk, c <= r)
    tri = jnp.where(causal, 1.0, 0.0).astype(BF16)
    blk = jnp.where(same_chunk, 1.0, 0.0).astype(BF16)
    la_hi, la_lo = _split_bf16(log_a)
    bcum = (jnp.dot(tri, la_hi, preferred_element_type=F32)
            + jnp.dot(tri, la_lo, preferred_element_type=F32))
    btot = (jnp.dot(blk, la_hi, preferred_element_type=F32)
            + jnp.dot(blk, la_lo, preferred_element_type=F32))

    q_dec = (gq * (GLA_KDIM ** -0.5)) * jnp.exp(bcum)
    k_inv = (gk * jnp.exp(-bcum)).astype(BF16)
    k_end = (gk * jnp.exp(btot - bcum)).astype(BF16)
    decay = jnp.exp(btot)
    gv_b = gv.astype(BF16)

    klane = lax.broadcasted_iota(jnp.int32, (ts, GLA_KW), 1)
    vlane = lax.broadcasted_iota(jnp.int32, (ts, GLA_WIDTH), 1)
    q_dec_b = q_dec.astype(BF16)
    o = jnp.zeros((ts, GLA_WIDTH), F32)
    for h in range(GLA_HEADS):
        qh = jnp.where(klane // GLA_KDIM == h, q_dec_b, jnp.zeros_like(q_dec_b))
        att = lax.dot_general(qh, k_inv, (((1,), (1,)), ((), ())), preferred_element_type=F32)
        att = jnp.where(causal, att, 0.0).astype(BF16)
        vh = jnp.where(vlane // GLA_VDIM == h, gv_b, jnp.zeros_like(gv_b))
        o = o + jnp.dot(att, vh, preferred_element_type=F32)

    srow = lax.broadcasted_iota(jnp.int32, (GLA_WIDTH, GLA_KW), 0)
    scol = lax.broadcasted_iota(jnp.int32, (GLA_WIDTH, GLA_KW), 1)
    head_diag = (srow // GLA_VDIM) == (scol // GLA_KDIM)
    state = state_scr[...]
    o_inter = []
    for n in range(nchunk):
        sl = slice(n * CHUNK, (n + 1) * CHUNK)
        o_inter.append(lax.dot_general(q_dec_b[sl], state.astype(BF16), (((1,), (1,)), ((), ())),
                                       preferred_element_type=F32))
        kv_t = lax.dot_general(gv_b[sl], k_end[sl], (((0,), (0,)), ((), ())),
                               preferred_element_type=F32)
        state = state * decay[n * CHUNK:n * CHUNK + 1, :] + jnp.where(head_diag, kv_t, 0.0)
    state_scr[...] = state
    o = o + jnp.concatenate(o_inter, axis=0)

    hr = lax.broadcasted_iota(jnp.int32, (GLA_WIDTH, GLA_WIDTH), 0)
    hc = lax.broadcasted_iota(jnp.int32, (GLA_WIDTH, GLA_WIDTH), 1)
    ones_blk = jnp.where((hr // GLA_VDIM) == (hc // GLA_VDIM), 1.0, 0.0).astype(BF16)
    sq_hi, sq_lo = _split_bf16(o * o)
    ms = (jnp.dot(sq_hi, ones_blk, preferred_element_type=F32)
          + jnp.dot(sq_lo, ones_blk, preferred_element_type=F32)) * (1.0 / GLA_VDIM)
    o = o * lax.rsqrt(ms + EPS) * gng_ref[...]
    silu = gr / (1.0 + jnp.exp(-gr))
    out_ref[:, POOL_WIDTH:POOL_WIDTH + GLA_WIDTH] = (o * silu).astype(out_ref.dtype)


def _gla_pool(rest, wg2, bg, gng, pw, ps, batch, seq, ts):
    t = batch * seq
    ns = seq // ts
    return pl.pallas_call(
        functools.partial(_gla_pool_kernel, ts=ts),
        grid=(batch, ns),
        in_specs=[
            pl.BlockSpec((ts, REST_W), lambda b, s: (b * ns + s, 0)),
            _resident((LANES, GLA_KW)),
            _resident((1, GLA_KW)),
            _resident((1, GLA_WIDTH)),
            _resident((POOL_WIDTH, POOL_WIDTH)),
            _resident((1, POOL_WIDTH)),
        ],
        out_specs=pl.BlockSpec((ts, POOL_WIDTH + GLA_WIDTH), lambda b, s: (b * ns + s, 0)),
        out_shape=jax.ShapeDtypeStruct((t, POOL_WIDTH + GLA_WIDTH), BF16),
        scratch_shapes=[
            pltpu.VMEM((GLA_WIDTH, GLA_KW), F32),
            pltpu.VMEM((POOL_HALO + ts, POOL_WIDTH), F32),
        ],
        compiler_params=pltpu.CompilerParams(
            dimension_semantics=("parallel", "arbitrary"), vmem_limit_bytes=VMEM_LIMIT),
        name="gla_pool",
    )(rest, wg2, bg, gng, pw, ps)


def _out_mlp_kernel(h_ref, ypg_ref, yd_ref, wo_ref, g2_ref, w1_ref, w2_ref, gf_ref, o_ref,
                    *, final):
    npg = POOL_WIDTH + GLA_WIDTH
    h1 = (h_ref[...]
          + jnp.dot(ypg_ref[...], wo_ref[0:npg, :], preferred_element_type=F32)
          + jnp.dot(yd_ref[...], wo_ref[npg:D_MODEL, :], preferred_element_type=F32))
    ms = jnp.mean(h1 * h1, axis=-1, keepdims=True)
    z = (h1 * lax.rsqrt(ms + EPS) * g2_ref[...]).astype(BF16)
    a = jnp.maximum(jnp.dot(z, w1_ref[...], preferred_element_type=F32), 0.0)
    acc = h1 + jnp.dot((a * a).astype(BF16), w2_ref[...], preferred_element_type=F32)
    if final:
        ms = jnp.mean(acc * acc, axis=-1, keepdims=True)
        acc = acc * lax.rsqrt(ms + EPS) * gf_ref[...]
    o_ref[...] = acc


def _out_mlp(h, ypg, yd, wo, g2, w1, w2, gf, tm, final):
    t = h.shape[0]
    return pl.pallas_call(
        functools.partial(_out_mlp_kernel, final=final),
        grid=(t // tm,),
        in_specs=[
            pl.BlockSpec((tm, D_MODEL), lambda i: (i, 0)),
            pl.BlockSpec((tm, POOL_WIDTH + GLA_WIDTH), lambda i: (i, 0)),
            pl.BlockSpec((tm, DIFF_WIDTH), lambda i: (i, 0)),
            _resident((D_MODEL, D_MODEL)),
            _resident((1, D_MODEL)),
            _resident((D_MODEL, D_FF)),
            _resident((D_FF, D_MODEL)),
            _resident((1, D_MODEL)),
        ],
        out_specs=pl.BlockSpec((tm, D_MODEL), lambda i: (i, 0)),
        out_shape=jax.ShapeDtypeStruct((t, D_MODEL), F32),
        compiler_params=pltpu.CompilerParams(
            dimension_semantics=("parallel",), vmem_limit_bytes=VMEM_LIMIT),
        name="out_mlp",
    )(h, ypg, yd, wo, g2, w1, w2, gf)


def _prep_w_in(w):
    pool = w[:, 0:256]
    dq = w[:, 256:768] * (DIFF_QKDIM ** -0.5 * LOG2_E)
    dk = w[:, 768:1280]
    dv = w[:, 1280:1792]
    gq = w[:, 1792:1920]
    gk = w[:, 1920:2048]
    gv = w[:, 2048:2304]
    gr = w[:, 2304:2560]
    gg = w[:, 2560:2576]
    pad = jnp.zeros((D_MODEL, REST_W - REST_GG - GLA_GATE_RANK), w.dtype)
    return jnp.concatenate([dq, dk, dv, pool, gq, gk, gv, gr, gg, pad], axis=1).astype(BF16)


def _prep_w_out(w):
    return jnp.concatenate([w[0:256], w[768:1024], w[256:768]], axis=0).astype(BF16)


def _prep_pool_w(pw):
    out = jnp.zeros((POOL_WIDTH, POOL_WIDTH), F32)
    for g in range(len(POOL_WINDOWS)):
        out = out.at[g * POOL_GDIM:(g + 1) * POOL_GDIM, g * POOL_GDIM:(g + 1) * POOL_GDIM].set(pw[g])
    return out.astype(BF16)


def _tiles(seq, t):
    tm = math.gcd(t, 512)
    tq = math.gcd(seq, 512)
    ts = math.gcd(seq, 256)
    return tm, tq, ts


def kernel(x, norm1_g, w_in, pool_w, pool_scale, diff_lq1, diff_lk1, diff_lq2, diff_lk2, diff_norm_g, gla_w_gate2, gla_b_gate, gla_norm_g, w_out, norm2_g, w_mlp1, w_mlp2, final_norm_g):
    batch, seq, d = x.shape
    assert d == D_MODEL and seq % CHUNK == 0
    depth = w_in.shape[0]
    t = batch * seq
    tm, tq, ts = _tiles(seq, t)
    h = x.reshape(t, D_MODEL)
    gf = final_norm_g.reshape(1, D_MODEL)
    for l in range(depth):
        lambda_init = 0.8 - 0.6 * math.exp(-0.3 * l)
        qkv, rest = _inproj(h, norm1_g[l].reshape(1, D_MODEL), _prep_w_in(w_in[l]), tm)
        lam = _diff_lambda(diff_lq1[l], diff_lk1[l], diff_lq2[l], diff_lk2[l], lambda_init)
        y_diff = _diff_attention(qkv, lam, diff_norm_g[l].reshape(1, DIFF_WIDTH),
                                 batch, seq, tq, lambda_init)
        wg2 = jnp.zeros((LANES, GLA_KW), F32).at[0:GLA_GATE_RANK].set(gla_w_gate2[l])
        y_pg = _gla_pool(rest, wg2, gla_b_gate[l].reshape(1, GLA_KW),
                         gla_norm_g[l].reshape(1, GLA_WIDTH), _prep_pool_w(pool_w[l]),
                         pool_scale[l].reshape(1, POOL_WIDTH), batch, seq, ts)
        h = _out_mlp(h, y_pg, y_diff, _prep_w_out(w_out[l]), norm2_g[l].reshape(1, D_MODEL),
                     w_mlp1[l].astype(BF16), w_mlp2[l].astype(BF16), gf, tm,
                     final=(l == depth - 1))
    return h.reshape(batch, seq, D_MODEL)
```

```python
import functools
import math

import jax
import jax.numpy as jnp
from jax import lax
from jax.experimental import pallas as pl
from jax.experimental.pallas import tpu as pltpu

F32 = jnp.float32
BF16 = jnp.bfloat16

D_MODEL = 1024
CHUNK = 64
POOL_WIDTH = 256
POOL_WINDOWS = (2, 4, 8, 16)
POOL_GDIM = 64
POOL_HALO = 16
DIFF_HEADS = 4
DIFF_VDIM = 128
DIFF_QKDIM = 64
DIFF_WIDTH = 512
GLA_HEADS = 4
GLA_VDIM = 64
GLA_KDIM = 32
GLA_WIDTH = 256
GLA_KW = GLA_HEADS * GLA_KDIM
GLA_GATE_RANK = 16
GLA_GATE_TAU = 16.0
D_FF = 4096
EPS = 1e-6

LANES = 128
QKV_W = 3 * DIFF_WIDTH
REST_POOL = 0
REST_GQ = 256
REST_GK = 384
REST_GV = 512
REST_GR = 768
REST_GG = 1024
REST_W = 1152
W_IN_COLS = QKV_W + REST_W

VMEM_LIMIT = 56 * 1024 * 1024
NEG_BIG = -1e30
LOG2_E = 1.4426950408889634
ATTN_ROW_BLOCK = 512
ATTN_PAIRS_PER_TRIP = 4
GLA_BATCH_ROWS = 2


def _resident(shape):
    nd = len(shape)
    return pl.BlockSpec(shape, lambda *_: (0,) * nd, pipeline_mode=pl.Buffered(1))


def _split_bf16(a):
    hi = a.astype(BF16)
    lo = (a - hi.astype(F32)).astype(BF16)
    return hi, lo


def _inproj_kernel(h_ref, g_ref, w_ref, qkv_ref, rest_ref):
    h = h_ref[...]
    ms = jnp.mean(h * h, axis=-1, keepdims=True)
    u = (h * lax.rsqrt(ms + EPS) * g_ref[...]).astype(BF16)
    qkv_ref[...] = jnp.dot(u, w_ref[:, :QKV_W], preferred_element_type=F32).astype(BF16)
    rest_ref[...] = jnp.dot(u, w_ref[:, QKV_W:], preferred_element_type=F32)


def _inproj(h, g, w, tm):
    t = h.shape[0]
    return pl.pallas_call(
        _inproj_kernel,
        grid=(t // tm,),
        in_specs=[
            pl.BlockSpec((tm, D_MODEL), lambda i: (i, 0)),
            _resident((1, D_MODEL)),
            _resident((D_MODEL, W_IN_COLS)),
        ],
        out_specs=[
            pl.BlockSpec((tm, QKV_W), lambda i: (i, 0)),
            pl.BlockSpec((tm, REST_W), lambda i: (i, 0)),
        ],
        out_shape=[
            jax.ShapeDtypeStruct((t, QKV_W), BF16),
            jax.ShapeDtypeStruct((t, REST_W), F32),
        ],
        compiler_params=pltpu.CompilerParams(
            dimension_semantics=("parallel",), vmem_limit_bytes=VMEM_LIMIT),
        name="inproj",
    )(h, g, w)


def _attn_kernel(q_ref, k_ref, v_ref, lam_ref, g_ref, o_ref,
                 qs_scr, e_scr, ka_scr, va_scr, sa_scr, sb_scr, la_scr, lb_scr, m_scr, acc_scr,
                 *, tq, rb, nq, lambda_init):
    m_rows = 2 * tq
    nblk = m_rows // rb
    reps = tq // LANES
    nitems = nq * (nq + 1) // 2

    va_scr[:, 0:DIFF_VDIM] = v_ref[...]
    va_scr[:, DIFF_VDIM:2 * DIFF_VDIM] = jnp.ones((va_scr.shape[0], DIFF_VDIM), BF16)

    lane = lax.broadcasted_iota(jnp.int32, (tq, LANES), 1)

    def stack_q(i, carry):
        q = q_ref[pl.ds(pl.multiple_of(i * tq, tq), tq), :]
        zero = jnp.zeros_like(q)
        qs_scr[i, 0:tq, :] = jnp.where(lane < DIFF_QKDIM, q, zero)
        qs_scr[i, tq:m_rows, :] = jnp.where(lane >= DIFF_QKDIM, q, zero)
        return carry

    lax.fori_loop(0, nq, stack_q, 0)

    seq = k_ref.shape[0]
    ka_scr[:, 0:LANES] = k_ref[...]
    kchunk = (lax.broadcasted_iota(jnp.int32, (seq, LANES), 0) % tq) // CHUNK
    klane = lax.broadcasted_iota(jnp.int32, (seq, LANES), 1)
    ka_scr[:, LANES:2 * LANES] = jnp.where(kchunk > klane, 1.0, 0.0).astype(BF16)
    qchunk = (lax.broadcasted_iota(jnp.int32, (m_rows, LANES), 0) % tq) // CHUNK
    qlane = lax.broadcasted_iota(jnp.int32, (m_rows, LANES), 1)
    e_scr[0] = jnp.zeros((m_rows, LANES), BF16)
    e_scr[1] = jnp.where(qchunk == qlane, NEG_BIG, 0.0).astype(BF16)
    acc_scr[...] = jnp.zeros(acc_scr.shape, F32)

    def qk_block(item, s_dst, r):
        qi, j = item
        rows = slice(r * rb, (r + 1) * rb)
        k = ka_scr[pl.ds(pl.multiple_of(j * tq, tq), tq), :]
        q = jnp.concatenate([qs_scr[qi, rows, :], e_scr[(j == qi).astype(jnp.int32), rows, :]],
                            axis=1)
        s = lax.dot_general(q, k, (((1,), (1,)), ((), ())), preferred_element_type=F32)
        s_dst[0][rows, :] = s
        lane_max = s[:, 0:LANES]
        for c in range(1, reps):
            lane_max = jnp.maximum(lane_max, s[:, c * LANES:(c + 1) * LANES])
        s_dst[1][rows, :] = lane_max

    def softmax_pv_block(item, s_src, r):
        qi, j = item
        rows = slice(r * rb, (r + 1) * rb)
        m_prev = jnp.where(j == 0, -jnp.inf, m_scr[rows, :])
        m_next = jnp.maximum(m_prev, jnp.max(s_src[1][rows, :], axis=1, keepdims=True))
        alpha = jnp.exp2(m_prev - m_next)
        p = jnp.exp2(s_src[0][rows, :] - jnp.concatenate([m_next] * reps, axis=1))
        m_scr[rows, :] = m_next
        va = va_scr[pl.ds(pl.multiple_of(j * tq, tq), tq), :]
        acc_scr[qi, rows, :] = (jnp.concatenate([alpha, alpha], axis=1) * acc_scr[qi, rows, :]
                                + jnp.dot(p.astype(BF16), va, preferred_element_type=F32))

    def stage(item_next, s_next, item_cur, s_cur):
        for r in range(nblk):
            if item_next is not None:
                qk_block(item_next, s_next, r)
            softmax_pv_block(item_cur, s_cur, r)

    def following(item):
        qi, j = item
        last = j == qi
        qn = jnp.where(last, qi + 1, qi)
        jn = jnp.where(last, 0, j + 1)
        over = qn >= nq
        return jnp.where(over, qi, qn), jnp.where(over, j, jn)

    first = (jnp.int32(0), jnp.int32(0))
    buf_a = (sa_scr, la_scr)
    buf_b = (sb_scr, lb_scr)
    for r in range(nblk):
        qk_block(first, buf_a, r)

    def pair(item0):
        item1 = following(item0)
        item2 = following(item1)
        stage(item1, buf_b, item0, buf_a)
        stage(item2, buf_a, item1, buf_b)
        return item2

    def body(_, item):
        for _ in range(ATTN_PAIRS_PER_TRIP):
            item = pair(item)
        return item

    npairs = nitems // 2
    item = lax.fori_loop(0, npairs // ATTN_PAIRS_PER_TRIP, body, first)
    for _ in range(npairs % ATTN_PAIRS_PER_TRIP):
        item = pair(item)
    if nitems % 2:
        stage(None, None, item, buf_a)

    def finalize(i, carry):
        acc = acc_scr[i]
        out = acc[:, 0:DIFF_VDIM] / acc[:, DIFF_VDIM:2 * DIFF_VDIM]
        o = out[0:tq] - lam_ref[...] * out[tq:m_rows]
        ms = jnp.mean(o * o, axis=-1, keepdims=True)
        o = o * lax.rsqrt(ms + EPS) * g_ref[...] * (1.0 - lambda_init)
        o_ref[pl.ds(pl.multiple_of(i * tq, tq), tq), :] = o.astype(o_ref.dtype)
        return carry

    lax.fori_loop(0, nq, finalize, 0)


def _lambda_kernel(lq1_ref, lk1_ref, lq2_ref, lk2_ref, lam_ref, *, lambda_init):
    a = jnp.sum(lq1_ref[...] * lk1_ref[...], axis=-1, keepdims=True)
    b = jnp.sum(lq2_ref[...] * lk2_ref[...], axis=-1, keepdims=True)
    lam_ref[...] = jnp.broadcast_to(jnp.exp(a) - jnp.exp(b) + lambda_init, lam_ref.shape)


def _diff_lambda(lq1, lk1, lq2, lk2, lambda_init):
    args = [a.reshape(1, DIFF_QKDIM) for a in (lq1, lk1, lq2, lk2)]
    return pl.pallas_call(
        functools.partial(_lambda_kernel, lambda_init=lambda_init),
        out_shape=jax.ShapeDtypeStruct((1, LANES), F32),
        name="diff_lambda",
    )(*args)


def _diff_attention(qkv, lam, norm_g, batch, seq, tq, lambda_init):
    t = batch * seq
    nq = seq // tq
    kcol = DIFF_WIDTH // DIFF_VDIM
    vcol = 2 * DIFF_WIDTH // DIFF_VDIM
    return pl.pallas_call(
        functools.partial(_attn_kernel, tq=tq, rb=min(ATTN_ROW_BLOCK, 2 * tq), nq=nq,
                          lambda_init=lambda_init),
        grid=(batch, DIFF_HEADS),
        in_specs=[
            pl.BlockSpec((seq, DIFF_VDIM), lambda b, h: (b, h)),
            pl.BlockSpec((seq, DIFF_VDIM), lambda b, h: (b, kcol + h)),
            pl.BlockSpec((seq, DIFF_VDIM), lambda b, h: (b, vcol + h)),
            pl.BlockSpec((1, LANES), lambda b, h: (0, 0)),
            pl.BlockSpec((1, DIFF_VDIM), lambda b, h: (0, h)),
        ],
        out_specs=pl.BlockSpec((seq, DIFF_VDIM), lambda b, h: (b, h)),
        out_shape=jax.ShapeDtypeStruct((t, DIFF_WIDTH), BF16),
        scratch_shapes=[
            pltpu.VMEM((nq, 2 * tq, LANES), BF16),
            pltpu.VMEM((2, 2 * tq, LANES), BF16),
            pltpu.VMEM((seq, 2 * LANES), BF16),
            pltpu.VMEM((seq, 2 * DIFF_VDIM), BF16),
            pltpu.VMEM((2 * tq, tq), F32),
            pltpu.VMEM((2 * tq, tq), F32),
            pltpu.VMEM((2 * tq, LANES), F32),
            pltpu.VMEM((2 * tq, LANES), F32),
            pltpu.VMEM((2 * tq, LANES), F32),
            pltpu.VMEM((nq, 2 * tq, 2 * DIFF_VDIM), F32),
        ],
        compiler_params=pltpu.CompilerParams(
            dimension_semantics=("parallel", "parallel"), vmem_limit_bytes=VMEM_LIMIT),
        name="diff_attention",
    )(qkv, qkv, qkv, lam, norm_g)


def _gla_pool_kernel(rest_ref, wg2_ref, bg_ref, gng_ref, pw_ref, ps_ref, out_ref,
                     state_scr, halo_scr, *, ts, nb):
    st = pl.program_id(1)
    nchunk = ts // CHUNK
    rows = range(nb)

    @pl.when(st == 0)
    def _():
        state_scr[...] = jnp.zeros_like(state_scr)
        halo_scr[:, 0:POOL_HALO, :] = jnp.zeros((nb, POOL_HALO, POOL_WIDTH), F32)

    def dot(a, b):
        return jnp.dot(a, b, preferred_element_type=F32)

    def dot_nt(a, b):
        return lax.dot_general(a, b, (((1,), (1,)), ((), ())), preferred_element_type=F32)

    def dot_tn(a, b):
        return lax.dot_general(a, b, (((0,), (0,)), ((), ())), preferred_element_type=F32)

    x = [rest_ref[b, :, REST_POOL:REST_POOL + POOL_WIDTH] for b in rows]
    for b in rows:
        halo_scr[b, POOL_HALO:POOL_HALO + ts, :] = x[b]

    def window_sums(b):
        xf = halo_scr[b]
        xa = xf[:, 0:LANES]
        c2 = xa + pltpu.roll(xa, 1, axis=0)
        c4 = c2 + pltpu.roll(c2, 2, axis=0)
        xb = xf[:, LANES:2 * LANES]
        d2 = xb + pltpu.roll(xb, 1, axis=0)
        d4 = d2 + pltpu.roll(d2, 2, axis=0)
        c8 = d4 + pltpu.roll(d4, 4, axis=0)
        c16 = c8 + pltpu.roll(c8, 8, axis=0)
        cut = slice(POOL_HALO, POOL_HALO + ts)
        return xa[cut], xb[cut], c2[cut], c4[cut], c8[cut], c16[cut]

    sums = [window_sums(b) for b in rows]
    tpos = (st * ts + lax.broadcasted_iota(jnp.int32, (ts, LANES), 0) + 1).astype(F32)
    lane = lax.broadcasted_iota(jnp.int32, (ts, LANES), 1)
    first = lane < POOL_GDIM
    inv_a = jnp.where(first, 1.0 / jnp.minimum(tpos, 2.0), 1.0 / jnp.minimum(tpos, 4.0))
    inv_b = jnp.where(first, 1.0 / jnp.minimum(tpos, 8.0), 1.0 / jnp.minimum(tpos, 16.0))

    def pooled_of(xa, xb, c2, c4, c8, c16):
        mean_a = jnp.where(first, c2, c4) * inv_a
        mean_b = jnp.where(first, c8, c16) * inv_b
        return jnp.concatenate([mean_a - xa, mean_b - xb], axis=1).astype(BF16)

    pooled = [pooled_of(*s) for s in sums]
    pw = pw_ref[...]
    y_pool = [dot(p, pw) * ps_ref[...] for p in pooled]

    gq = [rest_ref[b, :, REST_GQ:REST_GQ + GLA_KW] for b in rows]
    gk = [rest_ref[b, :, REST_GK:REST_GK + GLA_KW] for b in rows]
    gv_b = [rest_ref[b, :, REST_GV:REST_GV + GLA_WIDTH].astype(BF16) for b in rows]
    gr = [rest_ref[b, :, REST_GR:REST_GR + GLA_WIDTH] for b in rows]
    gg = [rest_ref[b, :, REST_GG:REST_GG + LANES] for b in rows]

    w_hi, w_lo = _split_bf16(wg2_ref[...])
    g_split = [_split_bf16(v) for v in gg]
    xg = [dot(hi, w_hi) + dot(lo, w_hi) + dot(hi, w_lo) + bg_ref[...] for hi, lo in g_split]
    log_a = [(jnp.minimum(v, 0.0) - jnp.log(1.0 + jnp.exp(-jnp.abs(v)))) * (1.0 / GLA_GATE_TAU)
             for v in xg]

    r = lax.broadcasted_iota(jnp.int32, (ts, ts), 0)
    c = lax.broadcasted_iota(jnp.int32, (ts, ts), 1)
    same_chunk = (r // CHUNK) == (c // CHUNK)
    causal = jnp.logical_and(same_chunk, c <= r)
    tri = jnp.where(causal, 1.0, 0.0).astype(BF16)
    blk = jnp.where(same_chunk, 1.0, 0.0).astype(BF16)
    la_split = [_split_bf16(v) for v in log_a]
    bcum = [dot(tri, hi) + dot(tri, lo) for hi, lo in la_split]
    btot = [dot(blk, hi) + dot(blk, lo) for hi, lo in la_split]

    q_dec_b = [((q * (GLA_KDIM ** -0.5)) * jnp.exp(bc)).astype(BF16) for q, bc in zip(gq, bcum)]
    k_inv = [(k * jnp.exp(-bc)).astype(BF16) for k, bc in zip(gk, bcum)]
    k_end = [(k * jnp.exp(bt - bc)).astype(BF16) for k, bt, bc in zip(gk, btot, bcum)]
    decay = [jnp.exp(bt) for bt in btot]

    klane = lax.broadcasted_iota(jnp.int32, (ts, GLA_KW), 1)
    vlane = lax.broadcasted_iota(jnp.int32, (ts, GLA_WIDTH), 1)
    o = [None] * nb
    for h in range(GLA_HEADS):
        qh = [jnp.where(klane // GLA_KDIM == h, q, jnp.zeros_like(q)) for q in q_dec_b]
        att = [jnp.where(causal, dot_nt(q, k), 0.0).astype(BF16) for q, k in zip(qh, k_inv)]
        vh = [jnp.where(vlane // GLA_VDIM == h, v, jnp.zeros_like(v)) for v in gv_b]
        part = [dot(a, v) for a, v in zip(att, vh)]
        o = part if h == 0 else [a + b for a, b in zip(o, part)]

    srow = lax.broadcasted_iota(jnp.int32, (GLA_WIDTH, GLA_KW), 0)
    scol = lax.broadcasted_iota(jnp.int32, (GLA_WIDTH, GLA_KW), 1)
    head_diag = (srow // GLA_VDIM) == (scol // GLA_KDIM)
    state = [state_scr[b] for b in rows]
    o_inter = [[] for _ in rows]
    for n in range(nchunk):
        sl = slice(n * CHUNK, (n + 1) * CHUNK)
        for b in rows:
            o_inter[b].append(dot_nt(q_dec_b[b][sl], state[b].astype(BF16)))
        kv_t = [dot_tn(gv_b[b][sl], k_end[b][sl]) for b in rows]
        state = [state[b] * decay[b][n * CHUNK:n * CHUNK + 1, :] + jnp.where(head_diag, kv_t[b], 0.0)
                 for b in rows]
    o = [o[b] + jnp.concatenate(o_inter[b], axis=0) for b in rows]

    hr = lax.broadcasted_iota(jnp.int32, (GLA_WIDTH, GLA_WIDTH), 0)
    hc = lax.broadcasted_iota(jnp.int32, (GLA_WIDTH, GLA_WIDTH), 1)
    ones_blk = jnp.where((hr // GLA_VDIM) == (hc // GLA_VDIM), 1.0, 0.0).astype(BF16)
    sq_split = [_split_bf16(v * v) for v in o]
    ms = [(dot(hi, ones_blk) + dot(lo, ones_blk)) * (1.0 / GLA_VDIM) for hi, lo in sq_split]
    o = [v * lax.rsqrt(m + EPS) * gng_ref[...] for v, m in zip(o, ms)]
    silu = [g / (1.0 + jnp.exp(-g)) for g in gr]

    for b in rows:
        out_ref[b] = jnp.concatenate([y_pool[b], o[b] * silu[b]], axis=1).astype(out_ref.dtype)
        state_scr[b] = state[b]
        halo_scr[b, 0:POOL_HALO, :] = x[b][ts - POOL_HALO:ts, :]


def _gla_pool(rest, wg2, bg, gng, pw, ps, batch, seq, ts):
    t = batch * seq
    ns = seq // ts
    nb = math.gcd(batch, GLA_BATCH_ROWS)
    out = pl.pallas_call(
        functools.partial(_gla_pool_kernel, ts=ts, nb=nb),
        grid=(batch // nb, ns),
        in_specs=[
            pl.BlockSpec((nb, ts, REST_W), lambda b, s: (b, s, 0)),
            _resident((LANES, GLA_KW)),
            _resident((1, GLA_KW)),
            _resident((1, GLA_WIDTH)),
            _resident((POOL_WIDTH, POOL_WIDTH)),
            _resident((1, POOL_WIDTH)),
        ],
        out_specs=pl.BlockSpec((nb, ts, POOL_WIDTH + GLA_WIDTH), lambda b, s: (b, s, 0)),
        out_shape=jax.ShapeDtypeStruct((batch, seq, POOL_WIDTH + GLA_WIDTH), BF16),
        scratch_shapes=[
            pltpu.VMEM((nb, GLA_WIDTH, GLA_KW), F32),
            pltpu.VMEM((nb, POOL_HALO + ts, POOL_WIDTH), F32),
        ],
        compiler_params=pltpu.CompilerParams(
            dimension_semantics=("parallel", "arbitrary"), vmem_limit_bytes=VMEM_LIMIT),
        name="gla_pool",
    )(rest.reshape(batch, seq, REST_W), wg2, bg, gng, pw, ps)
    return out.reshape(t, POOL_WIDTH + GLA_WIDTH)


def _out_mlp_kernel(h_ref, ypg_ref, yd_ref, wo_ref, g2_ref, w1_ref, w2_ref, gf_ref, o_ref,
                    *, final):
    npg = POOL_WIDTH + GLA_WIDTH
    h1 = (h_ref[...]
          + jnp.dot(ypg_ref[...], wo_ref[0:npg, :], preferred_element_type=F32)
          + jnp.dot(yd_ref[...], wo_ref[npg:D_MODEL, :], preferred_element_type=F32))
    ms = jnp.mean(h1 * h1, axis=-1, keepdims=True)
    z = (h1 * lax.rsqrt(ms + EPS) * g2_ref[...]).astype(BF16)
    a = jnp.maximum(jnp.dot(z, w1_ref[...], preferred_element_type=F32), 0.0)
    acc = h1 + jnp.dot((a * a).astype(BF16), w2_ref[...], preferred_element_type=F32)
    if final:
        ms = jnp.mean(acc * acc, axis=-1, keepdims=True)
        acc = acc * lax.rsqrt(ms + EPS) * gf_ref[...]
    o_ref[...] = acc


def _out_mlp(h, ypg, yd, wo, g2, w1, w2, gf, tm, final):
    t = h.shape[0]
    return pl.pallas_call(
        functools.partial(_out_mlp_kernel, final=final),
        grid=(t // tm,),
        in_specs=[
            pl.BlockSpec((tm, D_MODEL), lambda i: (i, 0)),
            pl.BlockSpec((tm, POOL_WIDTH + GLA_WIDTH), lambda i: (i, 0)),
            pl.BlockSpec((tm, DIFF_WIDTH), lambda i: (i, 0)),
            _resident((D_MODEL, D_MODEL)),
            _resident((1, D_MODEL)),
            _resident((D_MODEL, D_FF)),
            _resident((D_FF, D_MODEL)),
            _resident((1, D_MODEL)),
        ],
        out_specs=pl.BlockSpec((tm, D_MODEL), lambda i: (i, 0)),
        out_shape=jax.ShapeDtypeStruct((t, D_MODEL), F32),
        compiler_params=pltpu.CompilerParams(
            dimension_semantics=("parallel",), vmem_limit_bytes=VMEM_LIMIT),
        name="out_mlp",
    )(h, ypg, yd, wo, g2, w1, w2, gf)


def _prep_w_in(w):
    pool = w[:, 0:256]
    dq = w[:, 256:768] * (DIFF_QKDIM ** -0.5 * LOG2_E)
    dk = w[:, 768:1280]
    dv = w[:, 1280:1792]
    gq = w[:, 1792:1920]
    gk = w[:, 1920:2048]
    gv = w[:, 2048:2304]
    gr = w[:, 2304:2560]
    gg = w[:, 2560:2576]
    pad = jnp.zeros((D_MODEL, REST_W - REST_GG - GLA_GATE_RANK), w.dtype)
    return jnp.concatenate([dq, dk, dv, pool, gq, gk, gv, gr, gg, pad], axis=1).astype(BF16)


def _prep_w_out(w):
    return jnp.concatenate([w[0:256], w[768:1024], w[256:768]], axis=0).astype(BF16)


def _prep_pool_w(pw):
    out = jnp.zeros((POOL_WIDTH, POOL_WIDTH), F32)
    for g in range(len(POOL_WINDOWS)):
        out = out.at[g * POOL_GDIM:(g + 1) * POOL_GDIM, g * POOL_GDIM:(g + 1) * POOL_GDIM].set(pw[g])
    return out.astype(BF16)


def _tiles(seq, t):
    tm = math.gcd(t, 512)
    tq = math.gcd(seq, 512)
    ts = math.gcd(seq, 256)
    return tm, tq, ts


def kernel(x, norm1_g, w_in, pool_w, pool_scale, diff_lq1, diff_lk1, diff_lq2, diff_lk2, diff_norm_g, gla_w_gate2, gla_b_gate, gla_norm_g, w_out, norm2_g, w_mlp1, w_mlp2, final_norm_g):
    batch, seq, d = x.shape
    assert d == D_MODEL and seq % CHUNK == 0
    depth = w_in.shape[0]
    t = batch * seq
    tm, tq, ts = _tiles(seq, t)
    h = x.reshape(t, D_MODEL)
    gf = final_norm_g.reshape(1, D_MODEL)
    for l in range(depth):
        lambda_init = 0.8 - 0.6 * math.exp(-0.3 * l)
        qkv, rest = _inproj(h, norm1_g[l].reshape(1, D_MODEL), _prep_w_in(w_in[l]), tm)
        lam = _diff_lambda(diff_lq1[l], diff_lk1[l], diff_lq2[l], diff_lk2[l], lambda_init)
        y_diff = _diff_attention(qkv, lam, diff_norm_g[l].reshape(1, DIFF_WIDTH),
                                 batch, seq, tq, lambda_init)
        wg2 = jnp.zeros((LANES, GLA_KW), F32).at[0:GLA_GATE_RANK].set(gla_w_gate2[l])
        y_pg = _gla_pool(rest, wg2, gla_b_gate[l].reshape(1, GLA_KW),
                         gla_norm_g[l].reshape(1, GLA_WIDTH), _prep_pool_w(pool_w[l]),
                         pool_scale[l].reshape(1, POOL_WIDTH), batch, seq, ts)
        h = _out_mlp(h, y_pg, y_diff, _prep_w_out(w_out[l]), norm2_g[l].reshape(1, D_MODEL),
                     w_mlp1[l].astype(BF16), w_mlp2[l].astype(BF16), gf, tm,
                     final=(l == depth - 1))
    return h.reshape(batch, seq, D_MODEL)
```

```python
import functools
import math

import jax
import jax.numpy as jnp
from jax import lax
from jax.experimental import pallas as pl
from jax.experimental.pallas import tpu as pltpu

F32 = jnp.float32
BF16 = jnp.bfloat16

D_MODEL = 1024
CHUNK = 64
POOL_WIDTH = 256
POOL_WINDOWS = (2, 4, 8, 16)
POOL_GDIM = 64
POOL_HALO = 16
DIFF_HEADS = 4
DIFF_VDIM = 128
DIFF_QKDIM = 64
DIFF_WIDTH = 512
GLA_HEADS = 4
GLA_VDIM = 64
GLA_KDIM = 32
GLA_WIDTH = 256
GLA_KW = GLA_HEADS * GLA_KDIM
GLA_GATE_RANK = 16
GLA_GATE_TAU = 16.0
D_FF = 4096
EPS = 1e-6

LANES = 128
QKV_W = 3 * DIFF_WIDTH
REST_POOL = 0
REST_GQ = 256
REST_GK = 384
REST_GV = 512
REST_GR = 768
REST_GG = 1024
REST_W = 1152
W_IN_COLS = QKV_W + REST_W

VMEM_LIMIT = 56 * 1024 * 1024
NEG_BIG = -1e30
LOG2_E = 1.4426950408889634
ATTN_ROW_BLOCK = 512
ATTN_PAIRS_PER_TRIP = 4
GLA_BATCH_ROWS = 4


def _resident(shape):
    nd = len(shape)
    return pl.BlockSpec(shape, lambda *_: (0,) * nd, pipeline_mode=pl.Buffered(1))


def _split_bf16(a):
    hi = a.astype(BF16)
    lo = (a - hi.astype(F32)).astype(BF16)
    return hi, lo


def _inproj_kernel(h_ref, g_ref, w_ref, qkv_ref, rest_ref):
    h = h_ref[...]
    ms = jnp.mean(h * h, axis=-1, keepdims=True)
    u = (h * lax.rsqrt(ms + EPS) * g_ref[...]).astype(BF16)
    qkv_ref[...] = jnp.dot(u, w_ref[:, :QKV_W], preferred_element_type=F32).astype(BF16)
    rest_ref[...] = jnp.dot(u, w_ref[:, QKV_W:], preferred_element_type=F32)


def _inproj(h, g, w, tm):
    t = h.shape[0]
    return pl.pallas_call(
        _inproj_kernel,
        grid=(t // tm,),
        in_specs=[
            pl.BlockSpec((tm, D_MODEL), lambda i: (i, 0)),
            _resident((1, D_MODEL)),
            _resident((D_MODEL, W_IN_COLS)),
        ],
        out_specs=[
            pl.BlockSpec((tm, QKV_W), lambda i: (i, 0)),
            pl.BlockSpec((tm, REST_W), lambda i: (i, 0)),
        ],
        out_shape=[
            jax.ShapeDtypeStruct((t, QKV_W), BF16),
            jax.ShapeDtypeStruct((t, REST_W), F32),
        ],
        compiler_params=pltpu.CompilerParams(
            dimension_semantics=("parallel",), vmem_limit_bytes=VMEM_LIMIT),
        name="inproj",
    )(h, g, w)


def _attn_kernel(q_ref, k_ref, v_ref, lam_ref, g_ref, o_ref,
                 qs_scr, e_scr, ka_scr, va_scr, sa_scr, sb_scr, la_scr, lb_scr, m_scr, acc_scr,
                 *, tq, rb, nq, lambda_init):
    m_rows = 2 * tq
    nblk = m_rows // rb
    reps = tq // LANES
    nitems = nq * (nq + 1) // 2

    va_scr[:, 0:DIFF_VDIM] = v_ref[...]
    va_scr[:, DIFF_VDIM:2 * DIFF_VDIM] = jnp.ones((va_scr.shape[0], DIFF_VDIM), BF16)

    lane = lax.broadcasted_iota(jnp.int32, (tq, LANES), 1)

    def stack_q(i, carry):
        q = q_ref[pl.ds(pl.multiple_of(i * tq, tq), tq), :]
        zero = jnp.zeros_like(q)
        qs_scr[i, 0:tq, :] = jnp.where(lane < DIFF_QKDIM, q, zero)
        qs_scr[i, tq:m_rows, :] = jnp.where(lane >= DIFF_QKDIM, q, zero)
        return carry

    lax.fori_loop(0, nq, stack_q, 0)

    seq = k_ref.shape[0]
    ka_scr[:, 0:LANES] = k_ref[...]
    kchunk = (lax.broadcasted_iota(jnp.int32, (seq, LANES), 0) % tq) // CHUNK
    klane = lax.broadcasted_iota(jnp.int32, (seq, LANES), 1)
    ka_scr[:, LANES:2 * LANES] = jnp.where(kchunk > klane, 1.0, 0.0).astype(BF16)
    qchunk = (lax.broadcasted_iota(jnp.int32, (m_rows, LANES), 0) % tq) // CHUNK
    qlane = lax.broadcasted_iota(jnp.int32, (m_rows, LANES), 1)
    e_scr[0] = jnp.zeros((m_rows, LANES), BF16)
    e_scr[1] = jnp.where(qchunk == qlane, NEG_BIG, 0.0).astype(BF16)
    acc_scr[...] = jnp.zeros(acc_scr.shape, F32)

    def qk_block(item, s_dst, r):
        qi, j = item
        rows = slice(r * rb, (r + 1) * rb)
        k = ka_scr[pl.ds(pl.multiple_of(j * tq, tq), tq), :]
        q = jnp.concatenate([qs_scr[qi, rows, :], e_scr[(j == qi).astype(jnp.int32), rows, :]],
                            axis=1)
        s = lax.dot_general(q, k, (((1,), (1,)), ((), ())), preferred_element_type=F32)
        s_dst[0][rows, :] = s
        lane_max = s[:, 0:LANES]
        for c in range(1, reps):
            lane_max = jnp.maximum(lane_max, s[:, c * LANES:(c + 1) * LANES])
        s_dst[1][rows, :] = lane_max

    def softmax_pv_block(item, s_src, r):
        qi, j = item
        rows = slice(r * rb, (r + 1) * rb)
        m_prev = jnp.where(j == 0, -jnp.inf, m_scr[rows, :])
        m_next = jnp.maximum(m_prev, jnp.max(s_src[1][rows, :], axis=1, keepdims=True))
        alpha = jnp.exp2(m_prev - m_next)
        p = jnp.exp2(s_src[0][rows, :] - jnp.concatenate([m_next] * reps, axis=1))
        m_scr[rows, :] = m_next
        va = va_scr[pl.ds(pl.multiple_of(j * tq, tq), tq), :]
        acc_scr[qi, rows, :] = (jnp.concatenate([alpha, alpha], axis=1) * acc_scr[qi, rows, :]
                                + jnp.dot(p.astype(BF16), va, preferred_element_type=F32))

    def stage(item_next, s_next, item_cur, s_cur):
        for r in range(nblk):
            if item_next is not None:
                qk_block(item_next, s_next, r)
            softmax_pv_block(item_cur, s_cur, r)

    def following(item):
        qi, j = item
        last = j == qi
        qn = jnp.where(last, qi + 1, qi)
        jn = jnp.where(last, 0, j + 1)
        over = qn >= nq
        return jnp.where(over, qi, qn), jnp.where(over, j, jn)

    first = (jnp.int32(0), jnp.int32(0))
    buf_a = (sa_scr, la_scr)
    buf_b = (sb_scr, lb_scr)
    for r in range(nblk):
        qk_block(first, buf_a, r)

    def pair(item0):
        item1 = following(item0)
        item2 = following(item1)
        stage(item1, buf_b, item0, buf_a)
        stage(item2, buf_a, item1, buf_b)
        return item2

    def body(_, item):
        for _ in range(ATTN_PAIRS_PER_TRIP):
            item = pair(item)
        return item

    npairs = nitems // 2
    item = lax.fori_loop(0, npairs // ATTN_PAIRS_PER_TRIP, body, first)
    for _ in range(npairs % ATTN_PAIRS_PER_TRIP):
        item = pair(item)
    if nitems % 2:
        stage(None, None, item, buf_a)

    def finalize(i, carry):
        acc = acc_scr[i]
        out = acc[:, 0:DIFF_VDIM] / acc[:, DIFF_VDIM:2 * DIFF_VDIM]
        o = out[0:tq] - lam_ref[...] * out[tq:m_rows]
        ms = jnp.mean(o * o, axis=-1, keepdims=True)
        o = o * lax.rsqrt(ms + EPS) * g_ref[...] * (1.0 - lambda_init)
        o_ref[pl.ds(pl.multiple_of(i * tq, tq), tq), :] = o.astype(o_ref.dtype)
        return carry

    lax.fori_loop(0, nq, finalize, 0)


def _lambda_kernel(lq1_ref, lk1_ref, lq2_ref, lk2_ref, lam_ref, *, lambda_init):
    a = jnp.sum(lq1_ref[...] * lk1_ref[...], axis=-1, keepdims=True)
    b = jnp.sum(lq2_ref[...] * lk2_ref[...], axis=-1, keepdims=True)
    lam_ref[...] = jnp.broadcast_to(jnp.exp(a) - jnp.exp(b) + lambda_init, lam_ref.shape)


def _diff_lambda(lq1, lk1, lq2, lk2, lambda_init):
    args = [a.reshape(1, DIFF_QKDIM) for a in (lq1, lk1, lq2, lk2)]
    return pl.pallas_call(
        functools.partial(_lambda_kernel, lambda_init=lambda_init),
        out_shape=jax.ShapeDtypeStruct((1, LANES), F32),
        name="diff_lambda",
    )(*args)


def _diff_attention(qkv, lam, norm_g, batch, seq, tq, lambda_init):
    t = batch * seq
    nq = seq // tq
    kcol = DIFF_WIDTH // DIFF_VDIM
    vcol = 2 * DIFF_WIDTH // DIFF_VDIM
    return pl.pallas_call(
        functools.partial(_attn_kernel, tq=tq, rb=min(ATTN_ROW_BLOCK, 2 * tq), nq=nq,
                          lambda_init=lambda_init),
        grid=(batch, DIFF_HEADS),
        in_specs=[
            pl.BlockSpec((seq, DIFF_VDIM), lambda b, h: (b, h)),
            pl.BlockSpec((seq, DIFF_VDIM), lambda b, h: (b, kcol + h)),
            pl.BlockSpec((seq, DIFF_VDIM), lambda b, h: (b, vcol + h)),
            pl.BlockSpec((1, LANES), lambda b, h: (0, 0)),
            pl.BlockSpec((1, DIFF_VDIM), lambda b, h: (0, h)),
        ],
        out_specs=pl.BlockSpec((seq, DIFF_VDIM), lambda b, h: (b, h)),
        out_shape=jax.ShapeDtypeStruct((t, DIFF_WIDTH), BF16),
        scratch_shapes=[
            pltpu.VMEM((nq, 2 * tq, LANES), BF16),
            pltpu.VMEM((2, 2 * tq, LANES), BF16),
            pltpu.VMEM((seq, 2 * LANES), BF16),
            pltpu.VMEM((seq, 2 * DIFF_VDIM), BF16),
            pltpu.VMEM((2 * tq, tq), F32),
            pltpu.VMEM((2 * tq, tq), F32),
            pltpu.VMEM((2 * tq, LANES), F32),
            pltpu.VMEM((2 * tq, LANES), F32),
            pltpu.VMEM((2 * tq, LANES), F32),
            pltpu.VMEM((nq, 2 * tq, 2 * DIFF_VDIM), F32),
        ],
        compiler_params=pltpu.CompilerParams(
            dimension_semantics=("parallel", "parallel"), vmem_limit_bytes=VMEM_LIMIT),
        name="diff_attention",
    )(qkv, qkv, qkv, lam, norm_g)


def _gla_pool_kernel(rest_ref, wg2_ref, bg_ref, gng_ref, pw_ref, ps_ref, out_ref,
                     state_scr, halo_scr, *, ts, nb):
    st = pl.program_id(1)
    nchunk = ts // CHUNK
    rows = range(nb)

    @pl.when(st == 0)
    def _():
        state_scr[...] = jnp.zeros_like(state_scr)
        halo_scr[:, 0:POOL_HALO, :] = jnp.zeros((nb, POOL_HALO, POOL_WIDTH), F32)

    def dot(a, b):
        return jnp.dot(a, b, preferred_element_type=F32)

    def dot_nt(a, b):
        return lax.dot_general(a, b, (((1,), (1,)), ((), ())), preferred_element_type=F32)

    def dot_tn(a, b):
        return lax.dot_general(a, b, (((0,), (0,)), ((), ())), preferred_element_type=F32)

    x = [rest_ref[b, :, REST_POOL:REST_POOL + POOL_WIDTH] for b in rows]
    for b in rows:
        halo_scr[b, POOL_HALO:POOL_HALO + ts, :] = x[b]

    def window_sums(b):
        xf = halo_scr[b]
        xa = xf[:, 0:LANES]
        c2 = xa + pltpu.roll(xa, 1, axis=0)
        c4 = c2 + pltpu.roll(c2, 2, axis=0)
        xb = xf[:, LANES:2 * LANES]
        d2 = xb + pltpu.roll(xb, 1, axis=0)
        d4 = d2 + pltpu.roll(d2, 2, axis=0)
        c8 = d4 + pltpu.roll(d4, 4, axis=0)
        c16 = c8 + pltpu.roll(c8, 8, axis=0)
        cut = slice(POOL_HALO, POOL_HALO + ts)
        return xa[cut], xb[cut], c2[cut], c4[cut], c8[cut], c16[cut]

    sums = [window_sums(b) for b in rows]
    tpos = (st * ts + lax.broadcasted_iota(jnp.int32, (ts, LANES), 0) + 1).astype(F32)
    lane = lax.broadcasted_iota(jnp.int32, (ts, LANES), 1)
    first = lane < POOL_GDIM
    inv_a = jnp.where(first, 1.0 / jnp.minimum(tpos, 2.0), 1.0 / jnp.minimum(tpos, 4.0))
    inv_b = jnp.where(first, 1.0 / jnp.minimum(tpos, 8.0), 1.0 / jnp.minimum(tpos, 16.0))

    def pooled_of(xa, xb, c2, c4, c8, c16):
        mean_a = jnp.where(first, c2, c4) * inv_a
        mean_b = jnp.where(first, c8, c16) * inv_b
        return jnp.concatenate([mean_a - xa, mean_b - xb], axis=1).astype(BF16)

    pooled = [pooled_of(*s) for s in sums]
    pw = pw_ref[...]
    y_pool = [dot(p, pw) * ps_ref[...] for p in pooled]

    gq = [rest_ref[b, :, REST_GQ:REST_GQ + GLA_KW] for b in rows]
    gk = [rest_ref[b, :, REST_GK:REST_GK + GLA_KW] for b in rows]
    gv_b = [rest_ref[b, :, REST_GV:REST_GV + GLA_WIDTH].astype(BF16) for b in rows]
    gr = [rest_ref[b, :, REST_GR:REST_GR + GLA_WIDTH] for b in rows]
    gg = [rest_ref[b, :, REST_GG:REST_GG + LANES] for b in rows]

    w_hi, w_lo = _split_bf16(wg2_ref[...])
    w_cat = jnp.concatenate([w_hi, w_lo], axis=1)
    g_split = [_split_bf16(v) for v in gg]
    xg = []
    for hi, lo in g_split:
        hh = dot(hi, w_cat)
        xg.append(hh[:, 0:GLA_KW] + hh[:, GLA_KW:2 * GLA_KW] + dot(lo, w_hi) + bg_ref[...])
    log_a = [(jnp.minimum(v, 0.0) - jnp.log(1.0 + jnp.exp(-jnp.abs(v)))) * (1.0 / GLA_GATE_TAU)
             for v in xg]

    r = lax.broadcasted_iota(jnp.int32, (ts, ts), 0)
    c = lax.broadcasted_iota(jnp.int32, (ts, ts), 1)
    same_chunk = (r // CHUNK) == (c // CHUNK)
    causal = jnp.logical_and(same_chunk, c <= r)
    tri = jnp.where(causal, 1.0, 0.0).astype(BF16)
    blk = jnp.where(same_chunk, 1.0, 0.0).astype(BF16)
    la_cat = [jnp.concatenate(_split_bf16(v), axis=1) for v in log_a]
    bcum2 = [dot(tri, v) for v in la_cat]
    btot2 = [dot(blk, v) for v in la_cat]
    bcum = [v[:, 0:GLA_KW] + v[:, GLA_KW:2 * GLA_KW] for v in bcum2]
    btot = [v[:, 0:GLA_KW] + v[:, GLA_KW:2 * GLA_KW] for v in btot2]

    q_dec_b = [((q * (GLA_KDIM ** -0.5)) * jnp.exp(bc)).astype(BF16) for q, bc in zip(gq, bcum)]
    k_inv = [(k * jnp.exp(-bc)).astype(BF16) for k, bc in zip(gk, bcum)]
    k_end = [(k * jnp.exp(bt - bc)).astype(BF16) for k, bt, bc in zip(gk, btot, bcum)]
    decay = [jnp.exp(bt) for bt in btot]

    klane = lax.broadcasted_iota(jnp.int32, (ts, GLA_KW), 1)
    vlane = lax.broadcasted_iota(jnp.int32, (ts, GLA_WIDTH), 1)
    o = [None] * nb
    for h in range(GLA_HEADS):
        qh = [jnp.where(klane // GLA_KDIM == h, q, jnp.zeros_like(q)) for q in q_dec_b]
        att = [jnp.where(causal, dot_nt(q, k), 0.0).astype(BF16) for q, k in zip(qh, k_inv)]
        vh = [jnp.where(vlane // GLA_VDIM == h, v, jnp.zeros_like(v)) for v in gv_b]
        part = [dot(a, v) for a, v in zip(att, vh)]
        o = part if h == 0 else [a + b for a, b in zip(o, part)]

    srow = lax.broadcasted_iota(jnp.int32, (GLA_WIDTH, GLA_KW), 0)
    scol = lax.broadcasted_iota(jnp.int32, (GLA_WIDTH, GLA_KW), 1)
    head_diag = (srow // GLA_VDIM) == (scol // GLA_KDIM)
    state = [state_scr[b] for b in rows]
    o_inter = [[] for _ in rows]
    for n in range(nchunk):
        sl = slice(n * CHUNK, (n + 1) * CHUNK)
        for b in rows:
            o_inter[b].append(dot_nt(q_dec_b[b][sl], state[b].astype(BF16)))
        kv_t = [dot_tn(gv_b[b][sl], k_end[b][sl]) for b in rows]
        state = [state[b] * decay[b][n * CHUNK:n * CHUNK + 1, :] + jnp.where(head_diag, kv_t[b], 0.0)
                 for b in rows]
    o = [o[b] + jnp.concatenate(o_inter[b], axis=0) for b in rows]

    hr = lax.broadcasted_iota(jnp.int32, (GLA_WIDTH, GLA_WIDTH), 0)
    hc = lax.broadcasted_iota(jnp.int32, (GLA_WIDTH, GLA_WIDTH), 1)
    ones_blk = jnp.where((hr // GLA_VDIM) == (hc // GLA_VDIM), 1.0, 0.0).astype(BF16)
    sq_split = [_split_bf16(v * v) for v in o]
    ms = [(dot(hi, ones_blk) + dot(lo, ones_blk)) * (1.0 / GLA_VDIM) for hi, lo in sq_split]
    o = [v * lax.rsqrt(m + EPS) * gng_ref[...] for v, m in zip(o, ms)]
    silu = [g / (1.0 + jnp.exp(-g)) for g in gr]

    for b in rows:
        out_ref[b] = jnp.concatenate([y_pool[b], o[b] * silu[b]], axis=1).astype(out_ref.dtype)
        state_scr[b] = state[b]
        halo_scr[b, 0:POOL_HALO, :] = x[b][ts - POOL_HALO:ts, :]


def _gla_pool(rest, wg2, bg, gng, pw, ps, batch, seq, ts):
    t = batch * seq
    ns = seq // ts
    nb = math.gcd(batch, GLA_BATCH_ROWS)
    out = pl.pallas_call(
        functools.partial(_gla_pool_kernel, ts=ts, nb=nb),
        grid=(batch // nb, ns),
        in_specs=[
            pl.BlockSpec((nb, ts, REST_W), lambda b, s: (b, s, 0)),
            _resident((LANES, GLA_KW)),
            _resident((1, GLA_KW)),
            _resident((1, GLA_WIDTH)),
            _resident((POOL_WIDTH, POOL_WIDTH)),
            _resident((1, POOL_WIDTH)),
        ],
        out_specs=pl.BlockSpec((nb, ts, POOL_WIDTH + GLA_WIDTH), lambda b, s: (b, s, 0)),
        out_shape=jax.ShapeDtypeStruct((batch, seq, POOL_WIDTH + GLA_WIDTH), BF16),
        scratch_shapes=[
            pltpu.VMEM((nb, GLA_WIDTH, GLA_KW), F32),
            pltpu.VMEM((nb, POOL_HALO + ts, POOL_WIDTH), F32),
        ],
        compiler_params=pltpu.CompilerParams(
            dimension_semantics=("parallel", "arbitrary"), vmem_limit_bytes=VMEM_LIMIT),
        name="gla_pool",
    )(rest.reshape(batch, seq, REST_W), wg2, bg, gng, pw, ps)
    return out.reshape(t, POOL_WIDTH + GLA_WIDTH)


def _out_mlp_kernel(h_ref, ypg_ref, yd_ref, wo_ref, g2_ref, w1_ref, w2_ref, gf_ref, o_ref,
                    *, final):
    npg = POOL_WIDTH + GLA_WIDTH
    h1 = (h_ref[...]
          + jnp.dot(ypg_ref[...], wo_ref[0:npg, :], preferred_element_type=F32)
          + jnp.dot(yd_ref[...], wo_ref[npg:D_MODEL, :], preferred_element_type=F32))
    ms = jnp.mean(h1 * h1, axis=-1, keepdims=True)
    z = (h1 * lax.rsqrt(ms + EPS) * g2_ref[...]).astype(BF16)
    a = jnp.maximum(jnp.dot(z, w1_ref[...], preferred_element_type=F32), 0.0)
    acc = h1 + jnp.dot((a * a).astype(BF16), w2_ref[...], preferred_element_type=F32)
    if final:
        ms = jnp.mean(acc * acc, axis=-1, keepdims=True)
        acc = acc * lax.rsqrt(ms + EPS) * gf_ref[...]
    o_ref[...] = acc


def _out_mlp(h, ypg, yd, wo, g2, w1, w2, gf, tm, final):
    t = h.shape[0]
    return pl.pallas_call(
        functools.partial(_out_mlp_kernel, final=final),
        grid=(t // tm,),
        in_specs=[
            pl.BlockSpec((tm, D_MODEL), lambda i: (i, 0)),
            pl.BlockSpec((tm, POOL_WIDTH + GLA_WIDTH), lambda i: (i, 0)),
            pl.BlockSpec((tm, DIFF_WIDTH), lambda i: (i, 0)),
            _resident((D_MODEL, D_MODEL)),
            _resident((1, D_MODEL)),
            _resident((D_MODEL, D_FF)),
            _resident((D_FF, D_MODEL)),
            _resident((1, D_MODEL)),
        ],
        out_specs=pl.BlockSpec((tm, D_MODEL), lambda i: (i, 0)),
        out_shape=jax.ShapeDtypeStruct((t, D_MODEL), F32),
        compiler_params=pltpu.CompilerParams(
            dimension_semantics=("parallel",), vmem_limit_bytes=VMEM_LIMIT),
        name="out_mlp",
    )(h, ypg, yd, wo, g2, w1, w2, gf)


def _prep_w_in(w):
    pool = w[:, 0:256]
    dq = w[:, 256:768] * (DIFF_QKDIM ** -0.5 * LOG2_E)
    dk = w[:, 768:1280]
    dv = w[:, 1280:1792]
    gq = w[:, 1792:1920]
    gk = w[:, 1920:2048]
    gv = w[:, 2048:2304]
    gr = w[:, 2304:2560]
    gg = w[:, 2560:2576]
    pad = jnp.zeros((D_MODEL, REST_W - REST_GG - GLA_GATE_RANK), w.dtype)
    return jnp.concatenate([dq, dk, dv, pool, gq, gk, gv, gr, gg, pad], axis=1).astype(BF16)


def _prep_w_out(w):
    return jnp.concatenate([w[0:256], w[768:1024], w[256:768]], axis=0).astype(BF16)


def _prep_pool_w(pw):
    out = jnp.zeros((POOL_WIDTH, POOL_WIDTH), F32)
    for g in range(len(POOL_WINDOWS)):
        out = out.at[g * POOL_GDIM:(g + 1) * POOL_GDIM, g * POOL_GDIM:(g + 1) * POOL_GDIM].set(pw[g])
    return out.astype(BF16)


def _tiles(seq, t):
    tm = math.gcd(t, 512)
    tq = math.gcd(seq, 512)
    ts = math.gcd(seq, 256)
    return tm, tq, ts


def kernel(x, norm1_g, w_in, pool_w, pool_scale, diff_lq1, diff_lk1, diff_lq2, diff_lk2, diff_norm_g, gla_w_gate2, gla_b_gate, gla_norm_g, w_out, norm2_g, w_mlp1, w_mlp2, final_norm_g):
    batch, seq, d = x.shape
    assert d == D_MODEL and seq % CHUNK == 0
    depth = w_in.shape[0]
    t = batch * seq
    tm, tq, ts = _tiles(seq, t)
    h = x.reshape(t, D_MODEL)
    gf = final_norm_g.reshape(1, D_MODEL)
    for l in range(depth):
        lambda_init = 0.8 - 0.6 * math.exp(-0.3 * l)
        qkv, rest = _inproj(h, norm1_g[l].reshape(1, D_MODEL), _prep_w_in(w_in[l]), tm)
        lam = _diff_lambda(diff_lq1[l], diff_lk1[l], diff_lq2[l], diff_lk2[l], lambda_init)
        y_diff = _diff_attention(qkv, lam, diff_norm_g[l].reshape(1, DIFF_WIDTH),
                                 batch, seq, tq, lambda_init)
        wg2 = jnp.zeros((LANES, GLA_KW), F32).at[0:GLA_GATE_RANK].set(gla_w_gate2[l])
        y_pg = _gla_pool(rest, wg2, gla_b_gate[l].reshape(1, GLA_KW),
                         gla_norm_g[l].reshape(1, GLA_WIDTH), _prep_pool_w(pool_w[l]),
                         pool_scale[l].reshape(1, POOL_WIDTH), batch, seq, ts)
        h = _out_mlp(h, y_pg, y_diff, _prep_w_out(w_out[l]), norm2_g[l].reshape(1, D_MODEL),
                     w_mlp1[l].astype(BF16), w_mlp2[l].astype(BF16), gf, tm,
                     final=(l == depth - 1))
    return h.reshape(batch, seq, D_MODEL)
```

```python
import functools
import math

import jax
import jax.numpy as jnp
from jax import lax
from jax.experimental import pallas as pl
from jax.experimental.pallas import tpu as pltpu

F32 = jnp.float32
BF16 = jnp.bfloat16

D_MODEL = 1024
CHUNK = 64
POOL_WIDTH = 256
POOL_WINDOWS = (2, 4, 8, 16)
POOL_GDIM = 64
POOL_HALO = 16
DIFF_HEADS = 4
DIFF_VDIM = 128
DIFF_QKDIM = 64
DIFF_WIDTH = 512
GLA_HEADS = 4
GLA_VDIM = 64
GLA_KDIM = 32
GLA_WIDTH = 256
GLA_KW = GLA_HEADS * GLA_KDIM
GLA_GATE_RANK = 16
GLA_GATE_TAU = 16.0
D_FF = 4096
EPS = 1e-6

LANES = 128
QKV_W = 3 * DIFF_WIDTH
REST_POOL = 0
REST_GQ = 256
REST_GK = 384
REST_GV = 512
REST_GR = 768
REST_GG = 1024
REST_W = 1152
W_IN_COLS = QKV_W + REST_W

VMEM_LIMIT = 56 * 1024 * 1024
NEG_BIG = -1e30
LOG2_E = 1.4426950408889634
ATTN_ROW_BLOCK = 512
ATTN_PAIRS_PER_TRIP = 4
MIX_BATCH_ROWS = 4
QKV_TILE = 256


def _resident(shape):
    nd = len(shape)
    return pl.BlockSpec(shape, lambda *_: (0,) * nd, pipeline_mode=pl.Buffered(1))


def _split_bf16(a):
    hi = a.astype(BF16)
    lo = (a - hi.astype(F32)).astype(BF16)
    return hi, lo


def _attn_kernel(q_ref, k_ref, v_ref, lam_ref, g_ref, o_ref,
                 qs_scr, e_scr, ka_scr, va_scr, sa_scr, sb_scr, la_scr, lb_scr, m_scr, acc_scr,
                 *, tq, rb, nq, lambda_init):
    m_rows = 2 * tq
    nblk = m_rows // rb
    reps = tq // LANES
    nitems = nq * (nq + 1) // 2

    va_scr[:, 0:DIFF_VDIM] = v_ref[...]
    va_scr[:, DIFF_VDIM:2 * DIFF_VDIM] = jnp.ones((va_scr.shape[0], DIFF_VDIM), BF16)

    lane = lax.broadcasted_iota(jnp.int32, (tq, LANES), 1)

    def stack_q(i, carry):
        q = q_ref[pl.ds(pl.multiple_of(i * tq, tq), tq), :]
        zero = jnp.zeros_like(q)
        qs_scr[i, 0:tq, :] = jnp.where(lane < DIFF_QKDIM, q, zero)
        qs_scr[i, tq:m_rows, :] = jnp.where(lane >= DIFF_QKDIM, q, zero)
        return carry

    lax.fori_loop(0, nq, stack_q, 0)

    seq = k_ref.shape[0]
    ka_scr[:, 0:LANES] = k_ref[...]
    kchunk = (lax.broadcasted_iota(jnp.int32, (seq, LANES), 0) % tq) // CHUNK
    klane = lax.broadcasted_iota(jnp.int32, (seq, LANES), 1)
    ka_scr[:, LANES:2 * LANES] = jnp.where(kchunk > klane, 1.0, 0.0).astype(BF16)
    qchunk = (lax.broadcasted_iota(jnp.int32, (m_rows, LANES), 0) % tq) // CHUNK
    qlane = lax.broadcasted_iota(jnp.int32, (m_rows, LANES), 1)
    e_scr[0] = jnp.zeros((m_rows, LANES), BF16)
    e_scr[1] = jnp.where(qchunk == qlane, NEG_BIG, 0.0).astype(BF16)
    acc_scr[...] = jnp.zeros(acc_scr.shape, F32)

    def qk_block(item, s_dst, r):
        qi, j = item
        rows = slice(r * rb, (r + 1) * rb)
        k = ka_scr[pl.ds(pl.multiple_of(j * tq, tq), tq), :]
        q = jnp.concatenate([qs_scr[qi, rows, :], e_scr[(j == qi).astype(jnp.int32), rows, :]],
                            axis=1)
        s = lax.dot_general(q, k, (((1,), (1,)), ((), ())), preferred_element_type=F32)
        s_dst[0][rows, :] = s
        lane_max = s[:, 0:LANES]
        for c in range(1, reps):
            lane_max = jnp.maximum(lane_max, s[:, c * LANES:(c + 1) * LANES])
        s_dst[1][rows, :] = lane_max

    def softmax_pv_block(item, s_src, r):
        qi, j = item
        rows = slice(r * rb, (r + 1) * rb)
        m_prev = jnp.where(j == 0, -jnp.inf, m_scr[rows, :])
        m_next = jnp.maximum(m_prev, jnp.max(s_src[1][rows, :], axis=1, keepdims=True))
        alpha = jnp.exp2(m_prev - m_next)
        p = jnp.exp2(s_src[0][rows, :] - jnp.concatenate([m_next] * reps, axis=1))
        m_scr[rows, :] = m_next
        va = va_scr[pl.ds(pl.multiple_of(j * tq, tq), tq), :]
        acc_scr[qi, rows, :] = (jnp.concatenate([alpha, alpha], axis=1) * acc_scr[qi, rows, :]
                                + jnp.dot(p.astype(BF16), va, preferred_element_type=F32))

    def stage(item_next, s_next, item_cur, s_cur):
        for r in range(nblk):
            if item_next is not None:
                qk_block(item_next, s_next, r)
            softmax_pv_block(item_cur, s_cur, r)

    def following(item):
        qi, j = item
        last = j == qi
        qn = jnp.where(last, qi + 1, qi)
        jn = jnp.where(last, 0, j + 1)
        over = qn >= nq
        return jnp.where(over, qi, qn), jnp.where(over, j, jn)

    first = (jnp.int32(0), jnp.int32(0))
    buf_a = (sa_scr, la_scr)
    buf_b = (sb_scr, lb_scr)
    for r in range(nblk):
        qk_block(first, buf_a, r)

    def pair(item0):
        item1 = following(item0)
        item2 = following(item1)
        stage(item1, buf_b, item0, buf_a)
        stage(item2, buf_a, item1, buf_b)
        return item2

    def body(_, item):
        for _ in range(ATTN_PAIRS_PER_TRIP):
            item = pair(item)
        return item

    npairs = nitems // 2
    item = lax.fori_loop(0, npairs // ATTN_PAIRS_PER_TRIP, body, first)
    for _ in range(npairs % ATTN_PAIRS_PER_TRIP):
        item = pair(item)
    if nitems % 2:
        stage(None, None, item, buf_a)

    def finalize(i, carry):
        acc = acc_scr[i]
        out = acc[:, 0:DIFF_VDIM] / acc[:, DIFF_VDIM:2 * DIFF_VDIM]
        o = out[0:tq] - lam_ref[...] * out[tq:m_rows]
        ms = jnp.mean(o * o, axis=-1, keepdims=True)
        o = o * lax.rsqrt(ms + EPS) * g_ref[...] * (1.0 - lambda_init)
        o_ref[pl.ds(pl.multiple_of(i * tq, tq), tq), :] = o.astype(o_ref.dtype)
        return carry

    lax.fori_loop(0, nq, finalize, 0)


def _lambda_kernel(lq1_ref, lk1_ref, lq2_ref, lk2_ref, lam_ref, *, lambda_init):
    a = jnp.sum(lq1_ref[...] * lk1_ref[...], axis=-1, keepdims=True)
    b = jnp.sum(lq2_ref[...] * lk2_ref[...], axis=-1, keepdims=True)
    lam_ref[...] = jnp.broadcast_to(jnp.exp(a) - jnp.exp(b) + lambda_init, lam_ref.shape)


def _diff_lambda(lq1, lk1, lq2, lk2, lambda_init):
    args = [a.reshape(1, DIFF_QKDIM) for a in (lq1, lk1, lq2, lk2)]
    return pl.pallas_call(
        functools.partial(_lambda_kernel, lambda_init=lambda_init),
        out_shape=jax.ShapeDtypeStruct((1, LANES), F32),
        name="diff_lambda",
    )(*args)


def _diff_attention(qkv, lam, norm_g, batch, seq, tq, lambda_init):
    t = batch * seq
    nq = seq // tq
    kcol = DIFF_WIDTH // DIFF_VDIM
    vcol = 2 * DIFF_WIDTH // DIFF_VDIM
    return pl.pallas_call(
        functools.partial(_attn_kernel, tq=tq, rb=min(ATTN_ROW_BLOCK, 2 * tq), nq=nq,
                          lambda_init=lambda_init),
        grid=(batch, DIFF_HEADS),
        in_specs=[
            pl.BlockSpec((seq, DIFF_VDIM), lambda b, h: (b, h)),
            pl.BlockSpec((seq, DIFF_VDIM), lambda b, h: (b, kcol + h)),
            pl.BlockSpec((seq, DIFF_VDIM), lambda b, h: (b, vcol + h)),
            pl.BlockSpec((1, LANES), lambda b, h: (0, 0)),
            pl.BlockSpec((1, DIFF_VDIM), lambda b, h: (0, h)),
        ],
        out_specs=pl.BlockSpec((seq, DIFF_VDIM), lambda b, h: (b, h)),
        out_shape=jax.ShapeDtypeStruct((t, DIFF_WIDTH), BF16),
        scratch_shapes=[
            pltpu.VMEM((nq, 2 * tq, LANES), BF16),
            pltpu.VMEM((2, 2 * tq, LANES), BF16),
            pltpu.VMEM((seq, 2 * LANES), BF16),
            pltpu.VMEM((seq, 2 * DIFF_VDIM), BF16),
            pltpu.VMEM((2 * tq, tq), F32),
            pltpu.VMEM((2 * tq, tq), F32),
            pltpu.VMEM((2 * tq, LANES), F32),
            pltpu.VMEM((2 * tq, LANES), F32),
            pltpu.VMEM((2 * tq, LANES), F32),
            pltpu.VMEM((nq, 2 * tq, 2 * DIFF_VDIM), F32),
        ],
        compiler_params=pltpu.CompilerParams(
            dimension_semantics=("parallel", "parallel"), vmem_limit_bytes=VMEM_LIMIT),
        name="diff_attention",
    )(qkv, qkv, qkv, lam, norm_g)


def _inproj_mix_kernel(h_ref, g_ref, w_ref, wg2_ref, bg_ref, gng_ref, pw_ref, ps_ref,
                       qkv_ref, out_ref, state_scr, halo_scr, *, ts, nb):
    st = pl.program_id(1)
    nchunk = ts // CHUNK
    rows = range(nb)

    @pl.when(st == 0)
    def _():
        state_scr[...] = jnp.zeros_like(state_scr)
        halo_scr[:, 0:POOL_HALO, :] = jnp.zeros((nb, POOL_HALO, POOL_WIDTH), F32)

    def dot(a, b):
        return jnp.dot(a, b, preferred_element_type=F32)

    def dot_nt(a, b):
        return lax.dot_general(a, b, (((1,), (1,)), ((), ())), preferred_element_type=F32)

    def dot_tn(a, b):
        return lax.dot_general(a, b, (((0,), (0,)), ((), ())), preferred_element_type=F32)

    h = h_ref[...].reshape(nb * ts, D_MODEL)
    ms = jnp.mean(h * h, axis=-1, keepdims=True)
    u = (h * lax.rsqrt(ms + EPS) * g_ref[...]).astype(BF16)
    rest_all = dot(u, w_ref[:, QKV_W:])
    rest = [rest_all[b * ts:(b + 1) * ts] for b in rows]

    qkv_tiles = iter(range(QKV_W // QKV_TILE))

    def emit_qkv(count=1):
        for _ in range(count):
            c = next(qkv_tiles, None)
            if c is not None:
                cols = slice(c * QKV_TILE, (c + 1) * QKV_TILE)
                y = dot(u, w_ref[:, cols]).astype(BF16)
                for b in rows:
                    qkv_ref[b, :, cols] = y[b * ts:(b + 1) * ts]

    x = [rest[b][:, REST_POOL:REST_POOL + POOL_WIDTH] for b in rows]
    for b in rows:
        halo_scr[b, POOL_HALO:POOL_HALO + ts, :] = x[b]

    def window_sums(b):
        xf = halo_scr[b]
        xa = xf[:, 0:LANES]
        c2 = xa + pltpu.roll(xa, 1, axis=0)
        c4 = c2 + pltpu.roll(c2, 2, axis=0)
        xb = xf[:, LANES:2 * LANES]
        d2 = xb + pltpu.roll(xb, 1, axis=0)
        d4 = d2 + pltpu.roll(d2, 2, axis=0)
        c8 = d4 + pltpu.roll(d4, 4, axis=0)
        c16 = c8 + pltpu.roll(c8, 8, axis=0)
        cut = slice(POOL_HALO, POOL_HALO + ts)
        return xa[cut], xb[cut], c2[cut], c4[cut], c8[cut], c16[cut]

    sums = [window_sums(b) for b in rows]
    tpos = (st * ts + lax.broadcasted_iota(jnp.int32, (ts, LANES), 0) + 1).astype(F32)
    lane = lax.broadcasted_iota(jnp.int32, (ts, LANES), 1)
    first = lane < POOL_GDIM
    inv_a = jnp.where(first, 1.0 / jnp.minimum(tpos, 2.0), 1.0 / jnp.minimum(tpos, 4.0))
    inv_b = jnp.where(first, 1.0 / jnp.minimum(tpos, 8.0), 1.0 / jnp.minimum(tpos, 16.0))

    def pooled_of(xa, xb, c2, c4, c8, c16):
        mean_a = jnp.where(first, c2, c4) * inv_a
        mean_b = jnp.where(first, c8, c16) * inv_b
        return jnp.concatenate([mean_a - xa, mean_b - xb], axis=1).astype(BF16)

    pooled = [pooled_of(*s) for s in sums]
    pw = pw_ref[...]
    y_pool = [dot(p, pw) * ps_ref[...] for p in pooled]

    emit_qkv()
    gq = [rest[b][:, REST_GQ:REST_GQ + GLA_KW] for b in rows]
    gk = [rest[b][:, REST_GK:REST_GK + GLA_KW] for b in rows]
    gv_b = [rest[b][:, REST_GV:REST_GV + GLA_WIDTH].astype(BF16) for b in rows]
    gr = [rest[b][:, REST_GR:REST_GR + GLA_WIDTH] for b in rows]
    gg = [rest[b][:, REST_GG:REST_GG + LANES] for b in rows]

    w_hi, w_lo = _split_bf16(wg2_ref[...])
    w_cat = jnp.concatenate([w_hi, w_lo], axis=1)
    g_split = [_split_bf16(v) for v in gg]
    xg = []
    for hi, lo in g_split:
        hh = dot(hi, w_cat)
        xg.append(hh[:, 0:GLA_KW] + hh[:, GLA_KW:2 * GLA_KW] + dot(lo, w_hi) + bg_ref[...])
    log_a = [(jnp.minimum(v, 0.0) - jnp.log(1.0 + jnp.exp(-jnp.abs(v)))) * (1.0 / GLA_GATE_TAU)
             for v in xg]

    emit_qkv()
    r = lax.broadcasted_iota(jnp.int32, (ts, ts), 0)
    c = lax.broadcasted_iota(jnp.int32, (ts, ts), 1)
    same_chunk = (r // CHUNK) == (c // CHUNK)
    causal = jnp.logical_and(same_chunk, c <= r)
    tri = jnp.where(causal, 1.0, 0.0).astype(BF16)
    blk = jnp.where(same_chunk, 1.0, 0.0).astype(BF16)
    la_cat = [jnp.concatenate(_split_bf16(v), axis=1) for v in log_a]
    bcum2 = [dot(tri, v) for v in la_cat]
    btot2 = [dot(blk, v) for v in la_cat]
    bcum = [v[:, 0:GLA_KW] + v[:, GLA_KW:2 * GLA_KW] for v in bcum2]
    btot = [v[:, 0:GLA_KW] + v[:, GLA_KW:2 * GLA_KW] for v in btot2]

    q_dec_b = [((q * (GLA_KDIM ** -0.5)) * jnp.exp(bc)).astype(BF16) for q, bc in zip(gq, bcum)]
    k_inv = [(k * jnp.exp(-bc)).astype(BF16) for k, bc in zip(gk, bcum)]
    k_end = [(k * jnp.exp(bt - bc)).astype(BF16) for k, bt, bc in zip(gk, btot, bcum)]
    decay = [jnp.exp(bt) for bt in btot]

    emit_qkv()
    klane = lax.broadcasted_iota(jnp.int32, (ts, GLA_KW), 1)
    vlane = lax.broadcasted_iota(jnp.int32, (ts, GLA_WIDTH), 1)
    o = [None] * nb
    for h in range(GLA_HEADS):
        qh = [jnp.where(klane // GLA_KDIM == h, q, jnp.zeros_like(q)) for q in q_dec_b]
        att = [jnp.where(causal, dot_nt(q, k), 0.0).astype(BF16) for q, k in zip(qh, k_inv)]
        vh = [jnp.where(vlane // GLA_VDIM == h, v, jnp.zeros_like(v)) for v in gv_b]
        part = [dot(a, v) for a, v in zip(att, vh)]
        o = part if h == 0 else [a + b for a, b in zip(o, part)]
        if h % 2 == 1:
            emit_qkv()

    srow = lax.broadcasted_iota(jnp.int32, (GLA_WIDTH, GLA_KW), 0)
    scol = lax.broadcasted_iota(jnp.int32, (GLA_WIDTH, GLA_KW), 1)
    head_diag = (srow // GLA_VDIM) == (scol // GLA_KDIM)
    state = [state_scr[b] for b in rows]
    o_inter = [[] for _ in rows]
    for n in range(nchunk):
        sl = slice(n * CHUNK, (n + 1) * CHUNK)
        for b in rows:
            o_inter[b].append(dot_nt(q_dec_b[b][sl], state[b].astype(BF16)))
        kv_t = [dot_tn(gv_b[b][sl], k_end[b][sl]) for b in rows]
        state = [state[b] * decay[b][n * CHUNK:n * CHUNK + 1, :] + jnp.where(head_diag, kv_t[b], 0.0)
                 for b in rows]
    o = [o[b] + jnp.concatenate(o_inter[b], axis=0) for b in rows]

    emit_qkv()
    hr = lax.broadcasted_iota(jnp.int32, (GLA_WIDTH, GLA_WIDTH), 0)
    hc = lax.broadcasted_iota(jnp.int32, (GLA_WIDTH, GLA_WIDTH), 1)
    ones_blk = jnp.where((hr // GLA_VDIM) == (hc // GLA_VDIM), 1.0, 0.0).astype(BF16)
    sq_split = [_split_bf16(v * v) for v in o]
    ms = [(dot(hi, ones_blk) + dot(lo, ones_blk)) * (1.0 / GLA_VDIM) for hi, lo in sq_split]
    o = [v * lax.rsqrt(m + EPS) * gng_ref[...] for v, m in zip(o, ms)]
    silu = [g / (1.0 + jnp.exp(-g)) for g in gr]

    emit_qkv(QKV_W // QKV_TILE)
    for b in rows:
        out_ref[b] = jnp.concatenate([y_pool[b], o[b] * silu[b]], axis=1).astype(out_ref.dtype)
        state_scr[b] = state[b]
        halo_scr[b, 0:POOL_HALO, :] = x[b][ts - POOL_HALO:ts, :]


def _inproj_mix(h, g, w, wg2, bg, gng, pw, ps, batch, seq, ts):
    t = batch * seq
    nb = math.gcd(batch, MIX_BATCH_ROWS)
    qkv, y_pg = pl.pallas_call(
        functools.partial(_inproj_mix_kernel, ts=ts, nb=nb),
        grid=(batch // nb, seq // ts),
        in_specs=[
            pl.BlockSpec((nb, ts, D_MODEL), lambda b, s: (b, s, 0)),
            _resident((1, D_MODEL)),
            _resident((D_MODEL, W_IN_COLS)),
            _resident((LANES, GLA_KW)),
            _resident((1, GLA_KW)),
            _resident((1, GLA_WIDTH)),
            _resident((POOL_WIDTH, POOL_WIDTH)),
            _resident((1, POOL_WIDTH)),
        ],
        out_specs=[
            pl.BlockSpec((nb, ts, QKV_W), lambda b, s: (b, s, 0)),
            pl.BlockSpec((nb, ts, POOL_WIDTH + GLA_WIDTH), lambda b, s: (b, s, 0)),
        ],
        out_shape=[
            jax.ShapeDtypeStruct((batch, seq, QKV_W), BF16),
            jax.ShapeDtypeStruct((batch, seq, POOL_WIDTH + GLA_WIDTH), BF16),
        ],
        scratch_shapes=[
            pltpu.VMEM((nb, GLA_WIDTH, GLA_KW), F32),
            pltpu.VMEM((nb, POOL_HALO + ts, POOL_WIDTH), F32),
        ],
        compiler_params=pltpu.CompilerParams(
            dimension_semantics=("parallel", "arbitrary"), vmem_limit_bytes=VMEM_LIMIT),
        name="inproj_mix",
    )(h.reshape(batch, seq, D_MODEL), g, w, wg2, bg, gng, pw, ps)
    return qkv.reshape(t, QKV_W), y_pg.reshape(t, POOL_WIDTH + GLA_WIDTH)


def _out_mlp_kernel(h_ref, ypg_ref, yd_ref, wo_ref, g2_ref, w1_ref, w2_ref, gf_ref, o_ref,
                    *, final):
    npg = POOL_WIDTH + GLA_WIDTH
    h1 = (h_ref[...]
          + jnp.dot(ypg_ref[...], wo_ref[0:npg, :], preferred_element_type=F32)
          + jnp.dot(yd_ref[...], wo_ref[npg:D_MODEL, :], preferred_element_type=F32))
    ms = jnp.mean(h1 * h1, axis=-1, keepdims=True)
    z = (h1 * lax.rsqrt(ms + EPS) * g2_ref[...]).astype(BF16)
    a = jnp.maximum(jnp.dot(z, w1_ref[...], preferred_element_type=F32), 0.0)
    acc = h1 + jnp.dot((a * a).astype(BF16), w2_ref[...], preferred_element_type=F32)
    if final:
        ms = jnp.mean(acc * acc, axis=-1, keepdims=True)
        acc = acc * lax.rsqrt(ms + EPS) * gf_ref[...]
    o_ref[...] = acc


def _out_mlp(h, ypg, yd, wo, g2, w1, w2, gf, tm, final):
    t = h.shape[0]
    return pl.pallas_call(
        functools.partial(_out_mlp_kernel, final=final),
        grid=(t // tm,),
        in_specs=[
            pl.BlockSpec((tm, D_MODEL), lambda i: (i, 0)),
            pl.BlockSpec((tm, POOL_WIDTH + GLA_WIDTH), lambda i: (i, 0)),
            pl.BlockSpec((tm, DIFF_WIDTH), lambda i: (i, 0)),
            _resident((D_MODEL, D_MODEL)),
            _resident((1, D_MODEL)),
            _resident((D_MODEL, D_FF)),
            _resident((D_FF, D_MODEL)),
            _resident((1, D_MODEL)),
        ],
        out_specs=pl.BlockSpec((tm, D_MODEL), lambda i: (i, 0)),
        out_shape=jax.ShapeDtypeStruct((t, D_MODEL), F32),
        compiler_params=pltpu.CompilerParams(
            dimension_semantics=("parallel",), vmem_limit_bytes=VMEM_LIMIT),
        name="out_mlp",
    )(h, ypg, yd, wo, g2, w1, w2, gf)


def _prep_w_in(w):
    pool = w[:, 0:256]
    dq = w[:, 256:768] * (DIFF_QKDIM ** -0.5 * LOG2_E)
    dk = w[:, 768:1280]
    dv = w[:, 1280:1792]
    gq = w[:, 1792:1920]
    gk = w[:, 1920:2048]
    gv = w[:, 2048:2304]
    gr = w[:, 2304:2560]
    gg = w[:, 2560:2576]
    pad = jnp.zeros((D_MODEL, REST_W - REST_GG - GLA_GATE_RANK), w.dtype)
    return jnp.concatenate([dq, dk, dv, pool, gq, gk, gv, gr, gg, pad], axis=1).astype(BF16)


def _prep_w_out(w):
    return jnp.concatenate([w[0:256], w[768:1024], w[256:768]], axis=0).astype(BF16)


def _prep_pool_w(pw):
    out = jnp.zeros((POOL_WIDTH, POOL_WIDTH), F32)
    for g in range(len(POOL_WINDOWS)):
        out = out.at[g * POOL_GDIM:(g + 1) * POOL_GDIM, g * POOL_GDIM:(g + 1) * POOL_GDIM].set(pw[g])
    return out.astype(BF16)


def _tiles(seq, t):
    tm = math.gcd(t, 512)
    tq = math.gcd(seq, 512)
    ts = math.gcd(seq, 256)
    return tm, tq, ts


def kernel(x, norm1_g, w_in, pool_w, pool_scale, diff_lq1, diff_lk1, diff_lq2, diff_lk2, diff_norm_g, gla_w_gate2, gla_b_gate, gla_norm_g, w_out, norm2_g, w_mlp1, w_mlp2, final_norm_g):
    batch, seq, d = x.shape
    assert d == D_MODEL and seq % CHUNK == 0
    depth = w_in.shape[0]
    t = batch * seq
    tm, tq, ts = _tiles(seq, t)
    h = x.reshape(t, D_MODEL)
    gf = final_norm_g.reshape(1, D_MODEL)
    for l in range(depth):
        lambda_init = 0.8 - 0.6 * math.exp(-0.3 * l)
        wg2 = jnp.zeros((LANES, GLA_KW), F32).at[0:GLA_GATE_RANK].set(gla_w_gate2[l])
        qkv, y_pg = _inproj_mix(h, norm1_g[l].reshape(1, D_MODEL), _prep_w_in(w_in[l]), wg2,
                                gla_b_gate[l].reshape(1, GLA_KW),
                                gla_norm_g[l].reshape(1, GLA_WIDTH), _prep_pool_w(pool_w[l]),
                                pool_scale[l].reshape(1, POOL_WIDTH), batch, seq, ts)
        lam = _diff_lambda(diff_lq1[l], diff_lk1[l], diff_lq2[l], diff_lk2[l], lambda_init)
        y_diff = _diff_attention(qkv, lam, diff_norm_g[l].reshape(1, DIFF_WIDTH),
                                 batch, seq, tq, lambda_init)
        h = _out_mlp(h, y_pg, y_diff, _prep_w_out(w_out[l]), norm2_g[l].reshape(1, D_MODEL),
                     w_mlp1[l].astype(BF16), w_mlp2[l].astype(BF16), gf, tm,
                     final=(l == depth - 1))
    return h.reshape(batch, seq, D_MODEL)
```

```python
import functools
import math

import jax
import jax.numpy as jnp
from jax import lax
from jax.experimental import pallas as pl
from jax.experimental.pallas import tpu as pltpu

F32 = jnp.float32
BF16 = jnp.bfloat16

D_MODEL = 1024
CHUNK = 64
POOL_WIDTH = 256
POOL_WINDOWS = (2, 4, 8, 16)
POOL_GDIM = 64
POOL_HALO = 16
DIFF_HEADS = 4
DIFF_VDIM = 128
DIFF_QKDIM = 64
DIFF_WIDTH = 512
GLA_HEADS = 4
GLA_VDIM = 64
GLA_KDIM = 32
GLA_WIDTH = 256
GLA_KW = GLA_HEADS * GLA_KDIM
GLA_GATE_RANK = 16
GLA_GATE_TAU = 16.0
D_FF = 4096
EPS = 1e-6

LANES = 128
QKV_W = 3 * DIFF_WIDTH
REST_POOL = 0
REST_GQ = 256
REST_GK = 384
REST_GV = 512
REST_GR = 768
REST_GG = 1024
REST_W = 1152
W_IN_COLS = QKV_W + REST_W

VMEM_LIMIT = 56 * 1024 * 1024
NEG_BIG = -1e30
LOG2_E = 1.4426950408889634
ATTN_ROW_BLOCK = 512
ATTN_PAIRS_PER_TRIP = 8
MIX_BATCH_ROWS = 4
QKV_TILE = 256


def _resident(shape):
    nd = len(shape)
    return pl.BlockSpec(shape, lambda *_: (0,) * nd, pipeline_mode=pl.Buffered(1))


def _split_bf16(a):
    hi = a.astype(BF16)
    lo = (a - hi.astype(F32)).astype(BF16)
    return hi, lo


def _attn_kernel(q_ref, k_ref, v_ref, lam_ref, g_ref, o_ref,
                 qs_scr, e_scr, ka_scr, va_scr, sa_scr, sb_scr, la_scr, lb_scr, m_scr, acc_scr,
                 *, tq, rb, nq, lambda_init):
    m_rows = 2 * tq
    nblk = m_rows // rb
    reps = tq // LANES
    nitems = nq * (nq + 1) // 2

    va_scr[:, 0:DIFF_VDIM] = v_ref[...]
    va_scr[:, DIFF_VDIM:2 * DIFF_VDIM] = jnp.ones((va_scr.shape[0], DIFF_VDIM), BF16)

    lane = lax.broadcasted_iota(jnp.int32, (tq, LANES), 1)

    def stack_q(i, carry):
        q = q_ref[pl.ds(pl.multiple_of(i * tq, tq), tq), :]
        zero = jnp.zeros_like(q)
        qs_scr[i, 0:tq, :] = jnp.where(lane < DIFF_QKDIM, q, zero)
        qs_scr[i, tq:m_rows, :] = jnp.where(lane >= DIFF_QKDIM, q, zero)
        return carry

    lax.fori_loop(0, nq, stack_q, 0, unroll=math.gcd(nq, 4))

    seq = k_ref.shape[0]
    ka_scr[:, 0:LANES] = k_ref[...]
    kchunk = (lax.broadcasted_iota(jnp.int32, (seq, LANES), 0) % tq) // CHUNK
    klane = lax.broadcasted_iota(jnp.int32, (seq, LANES), 1)
    ka_scr[:, LANES:2 * LANES] = jnp.where(kchunk > klane, 1.0, 0.0).astype(BF16)
    qchunk = (lax.broadcasted_iota(jnp.int32, (m_rows, LANES), 0) % tq) // CHUNK
    qlane = lax.broadcasted_iota(jnp.int32, (m_rows, LANES), 1)
    e_scr[0] = jnp.zeros((m_rows, LANES), BF16)
    e_scr[1] = jnp.where(qchunk == qlane, NEG_BIG, 0.0).astype(BF16)
    acc_scr[...] = jnp.zeros(acc_scr.shape, F32)

    def qk_block(item, s_dst, r):
        qi, j = item
        rows = slice(r * rb, (r + 1) * rb)
        k = ka_scr[pl.ds(pl.multiple_of(j * tq, tq), tq), :]
        q = jnp.concatenate([qs_scr[qi, rows, :], e_scr[(j == qi).astype(jnp.int32), rows, :]],
                            axis=1)
        s = lax.dot_general(q, k, (((1,), (1,)), ((), ())), preferred_element_type=F32)
        s_dst[0][rows, :] = s
        lane_max = s[:, 0:LANES]
        for c in range(1, reps):
            lane_max = jnp.maximum(lane_max, s[:, c * LANES:(c + 1) * LANES])
        s_dst[1][rows, :] = lane_max

    def softmax_pv_block(item, s_src, r):
        qi, j = item
        rows = slice(r * rb, (r + 1) * rb)
        m_prev = jnp.where(j == 0, -jnp.inf, m_scr[rows, :])
        m_next = jnp.maximum(m_prev, jnp.max(s_src[1][rows, :], axis=1, keepdims=True))
        alpha = jnp.exp2(m_prev - m_next)
        p = jnp.exp2(s_src[0][rows, :] - jnp.concatenate([m_next] * reps, axis=1))
        m_scr[rows, :] = m_next
        va = va_scr[pl.ds(pl.multiple_of(j * tq, tq), tq), :]
        acc_scr[qi, rows, :] = (jnp.concatenate([alpha, alpha], axis=1) * acc_scr[qi, rows, :]
                                + jnp.dot(p.astype(BF16), va, preferred_element_type=F32))

    def stage(item_next, s_next, item_cur, s_cur):
        for r in range(nblk):
            if item_next is not None:
                qk_block(item_next, s_next, r)
            softmax_pv_block(item_cur, s_cur, r)

    def following(item):
        qi, j = item
        last = j == qi
        qn = jnp.where(last, qi + 1, qi)
        jn = jnp.where(last, 0, j + 1)
        over = qn >= nq
        return jnp.where(over, qi, qn), jnp.where(over, j, jn)

    first = (jnp.int32(0), jnp.int32(0))
    buf_a = (sa_scr, la_scr)
    buf_b = (sb_scr, lb_scr)
    for r in range(nblk):
        qk_block(first, buf_a, r)

    def pair(item0):
        item1 = following(item0)
        item2 = following(item1)
        stage(item1, buf_b, item0, buf_a)
        stage(item2, buf_a, item1, buf_b)
        return item2

    def body(_, item):
        for _ in range(ATTN_PAIRS_PER_TRIP):
            item = pair(item)
        return item

    npairs = nitems // 2
    item = lax.fori_loop(0, npairs // ATTN_PAIRS_PER_TRIP, body, first)
    for _ in range(npairs % ATTN_PAIRS_PER_TRIP):
        item = pair(item)
    if nitems % 2:
        stage(None, None, item, buf_a)

    def finalize(i, carry):
        acc = acc_scr[i]
        out = acc[:, 0:DIFF_VDIM] / acc[:, DIFF_VDIM:2 * DIFF_VDIM]
        o = out[0:tq] - lam_ref[...] * out[tq:m_rows]
        ms = jnp.mean(o * o, axis=-1, keepdims=True)
        o = o * lax.rsqrt(ms + EPS) * g_ref[...] * (1.0 - lambda_init)
        o_ref[pl.ds(pl.multiple_of(i * tq, tq), tq), :] = o.astype(o_ref.dtype)
        return carry

    lax.fori_loop(0, nq, finalize, 0, unroll=math.gcd(nq, 4))


def _lambda_kernel(lq1_ref, lk1_ref, lq2_ref, lk2_ref, lam_ref, *, lambda_init):
    a = jnp.sum(lq1_ref[...] * lk1_ref[...], axis=-1, keepdims=True)
    b = jnp.sum(lq2_ref[...] * lk2_ref[...], axis=-1, keepdims=True)
    lam_ref[...] = jnp.broadcast_to(jnp.exp(a) - jnp.exp(b) + lambda_init, lam_ref.shape)


def _diff_lambda(lq1, lk1, lq2, lk2, lambda_init):
    args = [a.reshape(1, DIFF_QKDIM) for a in (lq1, lk1, lq2, lk2)]
    return pl.pallas_call(
        functools.partial(_lambda_kernel, lambda_init=lambda_init),
        out_shape=jax.ShapeDtypeStruct((1, LANES), F32),
        name="diff_lambda",
    )(*args)


def _diff_attention(qkv, lam, norm_g, batch, seq, tq, lambda_init):
    t = batch * seq
    nq = seq // tq
    kcol = DIFF_WIDTH // DIFF_VDIM
    vcol = 2 * DIFF_WIDTH // DIFF_VDIM
    return pl.pallas_call(
        functools.partial(_attn_kernel, tq=tq, rb=min(ATTN_ROW_BLOCK, 2 * tq), nq=nq,
                          lambda_init=lambda_init),
        grid=(batch, DIFF_HEADS),
        in_specs=[
            pl.BlockSpec((seq, DIFF_VDIM), lambda b, h: (b, h)),
            pl.BlockSpec((seq, DIFF_VDIM), lambda b, h: (b, kcol + h)),
            pl.BlockSpec((seq, DIFF_VDIM), lambda b, h: (b, vcol + h)),
            pl.BlockSpec((1, LANES), lambda b, h: (0, 0)),
            pl.BlockSpec((1, DIFF_VDIM), lambda b, h: (0, h)),
        ],
        out_specs=pl.BlockSpec((seq, DIFF_VDIM), lambda b, h: (b, h)),
        out_shape=jax.ShapeDtypeStruct((t, DIFF_WIDTH), BF16),
        scratch_shapes=[
            pltpu.VMEM((nq, 2 * tq, LANES), BF16),
            pltpu.VMEM((2, 2 * tq, LANES), BF16),
            pltpu.VMEM((seq, 2 * LANES), BF16),
            pltpu.VMEM((seq, 2 * DIFF_VDIM), BF16),
            pltpu.VMEM((2 * tq, tq), F32),
            pltpu.VMEM((2 * tq, tq), F32),
            pltpu.VMEM((2 * tq, LANES), F32),
            pltpu.VMEM((2 * tq, LANES), F32),
            pltpu.VMEM((2 * tq, LANES), F32),
            pltpu.VMEM((nq, 2 * tq, 2 * DIFF_VDIM), F32),
        ],
        compiler_params=pltpu.CompilerParams(
            dimension_semantics=("parallel", "parallel"), vmem_limit_bytes=VMEM_LIMIT),
        name="diff_attention",
    )(qkv, qkv, qkv, lam, norm_g)


def _inproj_mix_kernel(h_ref, g_ref, w_ref, wg2_ref, bg_ref, gng_ref, pw_ref, ps_ref,
                       qkv_ref, out_ref, state_scr, halo_scr, *, ts, nb):
    st = pl.program_id(1)
    nchunk = ts // CHUNK
    rows = range(nb)

    @pl.when(st == 0)
    def _():
        state_scr[...] = jnp.zeros_like(state_scr)
        halo_scr[:, 0:POOL_HALO, :] = jnp.zeros((nb, POOL_HALO, POOL_WIDTH), F32)

    def dot(a, b):
        return jnp.dot(a, b, preferred_element_type=F32)

    def dot_nt(a, b):
        return lax.dot_general(a, b, (((1,), (1,)), ((), ())), preferred_element_type=F32)

    def dot_tn(a, b):
        return lax.dot_general(a, b, (((0,), (0,)), ((), ())), preferred_element_type=F32)

    h = h_ref[...].reshape(nb * ts, D_MODEL)
    ms = jnp.mean(h * h, axis=-1, keepdims=True)
    u = (h * lax.rsqrt(ms + EPS) * g_ref[...]).astype(BF16)
    rest_all = dot(u, w_ref[:, QKV_W:])
    rest = [rest_all[b * ts:(b + 1) * ts] for b in rows]

    qkv_tiles = iter(range(QKV_W // QKV_TILE))

    def emit_qkv(count=1):
        for _ in range(count):
            c = next(qkv_tiles, None)
            if c is not None:
                cols = slice(c * QKV_TILE, (c + 1) * QKV_TILE)
                y = dot(u, w_ref[:, cols]).astype(BF16)
                for b in rows:
                    qkv_ref[b, :, cols] = y[b * ts:(b + 1) * ts]

    x = [rest[b][:, REST_POOL:REST_POOL + POOL_WIDTH] for b in rows]
    for b in rows:
        halo_scr[b, POOL_HALO:POOL_HALO + ts, :] = x[b]

    def window_sums(b):
        xf = halo_scr[b]
        xa = xf[:, 0:LANES]
        c2 = xa + pltpu.roll(xa, 1, axis=0)
        c4 = c2 + pltpu.roll(c2, 2, axis=0)
        xb = xf[:, LANES:2 * LANES]
        d2 = xb + pltpu.roll(xb, 1, axis=0)
        d4 = d2 + pltpu.roll(d2, 2, axis=0)
        c8 = d4 + pltpu.roll(d4, 4, axis=0)
        c16 = c8 + pltpu.roll(c8, 8, axis=0)
        cut = slice(POOL_HALO, POOL_HALO + ts)
        return xa[cut], xb[cut], c2[cut], c4[cut], c8[cut], c16[cut]

    sums = [window_sums(b) for b in rows]
    tpos = (st * ts + lax.broadcasted_iota(jnp.int32, (ts, LANES), 0) + 1).astype(F32)
    lane = lax.broadcasted_iota(jnp.int32, (ts, LANES), 1)
    first = lane < POOL_GDIM
    inv_a = jnp.where(first, 1.0 / jnp.minimum(tpos, 2.0), 1.0 / jnp.minimum(tpos, 4.0))
    inv_b = jnp.where(first, 1.0 / jnp.minimum(tpos, 8.0), 1.0 / jnp.minimum(tpos, 16.0))

    def pooled_of(xa, xb, c2, c4, c8, c16):
        mean_a = jnp.where(first, c2, c4) * inv_a
        mean_b = jnp.where(first, c8, c16) * inv_b
        return jnp.concatenate([mean_a - xa, mean_b - xb], axis=1).astype(BF16)

    pooled = [pooled_of(*s) for s in sums]
    pw = pw_ref[...]
    y_pool = [dot(p, pw) * ps_ref[...] for p in pooled]

    emit_qkv()
    gq = [rest[b][:, REST_GQ:REST_GQ + GLA_KW] for b in rows]
    gk = [rest[b][:, REST_GK:REST_GK + GLA_KW] for b in rows]
    gv_b = [rest[b][:, REST_GV:REST_GV + GLA_WIDTH].astype(BF16) for b in rows]
    gr = [rest[b][:, REST_GR:REST_GR + GLA_WIDTH] for b in rows]
    gg = [rest[b][:, REST_GG:REST_GG + LANES] for b in rows]

    w_hi, w_lo = _split_bf16(wg2_ref[...])
    w_cat = jnp.concatenate([w_hi, w_lo], axis=1)
    g_split = [_split_bf16(v) for v in gg]
    xg = []
    for hi, lo in g_split:
        hh = dot(hi, w_cat)
        xg.append(hh[:, 0:GLA_KW] + hh[:, GLA_KW:2 * GLA_KW] + dot(lo, w_hi) + bg_ref[...])
    log_a = [(jnp.minimum(v, 0.0) - jnp.log(1.0 + jnp.exp(-jnp.abs(v)))) * (1.0 / GLA_GATE_TAU)
             for v in xg]

    emit_qkv()
    r = lax.broadcasted_iota(jnp.int32, (ts, ts), 0)
    c = lax.broadcasted_iota(jnp.int32, (ts, ts), 1)
    same_chunk = (r // CHUNK) == (c // CHUNK)
    causal = jnp.logical_and(same_chunk, c <= r)
    tri = jnp.where(causal, 1.0, 0.0).astype(BF16)
    blk = jnp.where(same_chunk, 1.0, 0.0).astype(BF16)
    la_cat = [jnp.concatenate(_split_bf16(v), axis=1) for v in log_a]
    bcum2 = [dot(tri, v) for v in la_cat]
    btot2 = [dot(blk, v) for v in la_cat]
    bcum = [v[:, 0:GLA_KW] + v[:, GLA_KW:2 * GLA_KW] for v in bcum2]
    btot = [v[:, 0:GLA_KW] + v[:, GLA_KW:2 * GLA_KW] for v in btot2]

    q_dec_b = [((q * (GLA_KDIM ** -0.5)) * jnp.exp(bc)).astype(BF16) for q, bc in zip(gq, bcum)]
    k_inv = [(k * jnp.exp(-bc)).astype(BF16) for k, bc in zip(gk, bcum)]
    k_end = [(k * jnp.exp(bt - bc)).astype(BF16) for k, bt, bc in zip(gk, btot, bcum)]
    decay = [jnp.exp(bt) for bt in btot]

    emit_qkv()
    klane = lax.broadcasted_iota(jnp.int32, (ts, GLA_KW), 1)
    vlane = lax.broadcasted_iota(jnp.int32, (ts, GLA_WIDTH), 1)
    o = [None] * nb
    for h in range(GLA_HEADS):
        qh = [jnp.where(klane // GLA_KDIM == h, q, jnp.zeros_like(q)) for q in q_dec_b]
        att = [jnp.where(causal, dot_nt(q, k), 0.0).astype(BF16) for q, k in zip(qh, k_inv)]
        vh = [jnp.where(vlane // GLA_VDIM == h, v, jnp.zeros_like(v)) for v in gv_b]
        part = [dot(a, v) for a, v in zip(att, vh)]
        o = part if h == 0 else [a + b for a, b in zip(o, part)]
        if h % 2 == 1:
            emit_qkv()

    srow = lax.broadcasted_iota(jnp.int32, (GLA_WIDTH, GLA_KW), 0)
    scol = lax.broadcasted_iota(jnp.int32, (GLA_WIDTH, GLA_KW), 1)
    head_diag = (srow // GLA_VDIM) == (scol // GLA_KDIM)
    state = [state_scr[b] for b in rows]
    o_inter = [[] for _ in rows]
    for n in range(nchunk):
        sl = slice(n * CHUNK, (n + 1) * CHUNK)
        for b in rows:
            o_inter[b].append(dot_nt(q_dec_b[b][sl], state[b].astype(BF16)))
        kv_t = [dot_tn(gv_b[b][sl], k_end[b][sl]) for b in rows]
        state = [state[b] * decay[b][n * CHUNK:n * CHUNK + 1, :] + jnp.where(head_diag, kv_t[b], 0.0)
                 for b in rows]
    o = [o[b] + jnp.concatenate(o_inter[b], axis=0) for b in rows]

    emit_qkv()
    hr = lax.broadcasted_iota(jnp.int32, (GLA_WIDTH, GLA_WIDTH), 0)
    hc = lax.broadcasted_iota(jnp.int32, (GLA_WIDTH, GLA_WIDTH), 1)
    ones_blk = jnp.where((hr // GLA_VDIM) == (hc // GLA_VDIM), 1.0, 0.0).astype(BF16)
    sq_split = [_split_bf16(v * v) for v in o]
    ms = [(dot(hi, ones_blk) + dot(lo, ones_blk)) * (1.0 / GLA_VDIM) for hi, lo in sq_split]
    o = [v * lax.rsqrt(m + EPS) * gng_ref[...] for v, m in zip(o, ms)]
    silu = [g / (1.0 + jnp.exp(-g)) for g in gr]

    emit_qkv(QKV_W // QKV_TILE)
    for b in rows:
        out_ref[b] = jnp.concatenate([y_pool[b], o[b] * silu[b]], axis=1).astype(out_ref.dtype)
        state_scr[b] = state[b]
        halo_scr[b, 0:POOL_HALO, :] = x[b][ts - POOL_HALO:ts, :]


def _inproj_mix(h, g, w, wg2, bg, gng, pw, ps, batch, seq, ts):
    t = batch * seq
    nb = math.gcd(batch, MIX_BATCH_ROWS)
    qkv, y_pg = pl.pallas_call(
        functools.partial(_inproj_mix_kernel, ts=ts, nb=nb),
        grid=(batch // nb, seq // ts),
        in_specs=[
            pl.BlockSpec((nb, ts, D_MODEL), lambda b, s: (b, s, 0)),
            _resident((1, D_MODEL)),
            _resident((D_MODEL, W_IN_COLS)),
            _resident((LANES, GLA_KW)),
            _resident((1, GLA_KW)),
            _resident((1, GLA_WIDTH)),
            _resident((POOL_WIDTH, POOL_WIDTH)),
            _resident((1, POOL_WIDTH)),
        ],
        out_specs=[
            pl.BlockSpec((nb, ts, QKV_W), lambda b, s: (b, s, 0)),
            pl.BlockSpec((nb, ts, POOL_WIDTH + GLA_WIDTH), lambda b, s: (b, s, 0)),
        ],
        out_shape=[
            jax.ShapeDtypeStruct((batch, seq, QKV_W), BF16),
            jax.ShapeDtypeStruct((batch, seq, POOL_WIDTH + GLA_WIDTH), BF16),
        ],
        scratch_shapes=[
            pltpu.VMEM((nb, GLA_WIDTH, GLA_KW), F32),
            pltpu.VMEM((nb, POOL_HALO + ts, POOL_WIDTH), F32),
        ],
        compiler_params=pltpu.CompilerParams(
            dimension_semantics=("parallel", "arbitrary"), vmem_limit_bytes=VMEM_LIMIT),
        name="inproj_mix",
    )(h.reshape(batch, seq, D_MODEL), g, w, wg2, bg, gng, pw, ps)
    return qkv.reshape(t, QKV_W), y_pg.reshape(t, POOL_WIDTH + GLA_WIDTH)


def _out_mlp_kernel(h_ref, ypg_ref, yd_ref, wo_ref, g2_ref, w1_ref, w2_ref, gf_ref, o_ref,
                    *, final):
    npg = POOL_WIDTH + GLA_WIDTH
    h1 = (h_ref[...]
          + jnp.dot(ypg_ref[...], wo_ref[0:npg, :], preferred_element_type=F32)
          + jnp.dot(yd_ref[...], wo_ref[npg:D_MODEL, :], preferred_element_type=F32))
    ms = jnp.mean(h1 * h1, axis=-1, keepdims=True)
    z = (h1 * lax.rsqrt(ms + EPS) * g2_ref[...]).astype(BF16)
    a = jnp.maximum(jnp.dot(z, w1_ref[...], preferred_element_type=F32), 0.0)
    acc = h1 + jnp.dot((a * a).astype(BF16), w2_ref[...], preferred_element_type=F32)
    if final:
        ms = jnp.mean(acc * acc, axis=-1, keepdims=True)
        acc = acc * lax.rsqrt(ms + EPS) * gf_ref[...]
    o_ref[...] = acc


def _out_mlp(h, ypg, yd, wo, g2, w1, w2, gf, tm, final):
    t = h.shape[0]
    return pl.pallas_call(
        functools.partial(_out_mlp_kernel, final=final),
        grid=(t // tm,),
        in_specs=[
            pl.BlockSpec((tm, D_MODEL), lambda i: (i, 0)),
            pl.BlockSpec((tm, POOL_WIDTH + GLA_WIDTH), lambda i: (i, 0)),
            pl.BlockSpec((tm, DIFF_WIDTH), lambda i: (i, 0)),
            _resident((D_MODEL, D_MODEL)),
            _resident((1, D_MODEL)),
            _resident((D_MODEL, D_FF)),
            _resident((D_FF, D_MODEL)),
            _resident((1, D_MODEL)),
        ],
        out_specs=pl.BlockSpec((tm, D_MODEL), lambda i: (i, 0)),
        out_shape=jax.ShapeDtypeStruct((t, D_MODEL), F32),
        compiler_params=pltpu.CompilerParams(
            dimension_semantics=("parallel",), vmem_limit_bytes=VMEM_LIMIT),
        name="out_mlp",
    )(h, ypg, yd, wo, g2, w1, w2, gf)


def _prep_w_in(w):
    pool = w[:, 0:256]
    dq = w[:, 256:768] * (DIFF_QKDIM ** -0.5 * LOG2_E)
    dk = w[:, 768:1280]
    dv = w[:, 1280:1792]
    gq = w[:, 1792:1920]
    gk = w[:, 1920:2048]
    gv = w[:, 2048:2304]
    gr = w[:, 2304:2560]
    gg = w[:, 2560:2576]
    pad = jnp.zeros((D_MODEL, REST_W - REST_GG - GLA_GATE_RANK), w.dtype)
    return jnp.concatenate([dq, dk, dv, pool, gq, gk, gv, gr, gg, pad], axis=1).astype(BF16)


def _prep_w_out(w):
    return jnp.concatenate([w[0:256], w[768:1024], w[256:768]], axis=0).astype(BF16)


def _prep_pool_w(pw):
    out = jnp.zeros((POOL_WIDTH, POOL_WIDTH), F32)
    for g in range(len(POOL_WINDOWS)):
        out = out.at[g * POOL_GDIM:(g + 1) * POOL_GDIM, g * POOL_GDIM:(g + 1) * POOL_GDIM].set(pw[g])
    return out.astype(BF16)


def _tiles(seq, t):
    tm = math.gcd(t, 512)
    tq = math.gcd(seq, 512)
    ts = math.gcd(seq, 256)
    return tm, tq, ts


def kernel(x, norm1_g, w_in, pool_w, pool_scale, diff_lq1, diff_lk1, diff_lq2, diff_lk2, diff_norm_g, gla_w_gate2, gla_b_gate, gla_norm_g, w_out, norm2_g, w_mlp1, w_mlp2, final_norm_g):
    batch, seq, d = x.shape
    assert d == D_MODEL and seq % CHUNK == 0
    depth = w_in.shape[0]
    t = batch * seq
    tm, tq, ts = _tiles(seq, t)
    h = x.reshape(t, D_MODEL)
    gf = final_norm_g.reshape(1, D_MODEL)
    for l in range(depth):
        lambda_init = 0.8 - 0.6 * math.exp(-0.3 * l)
        wg2 = jnp.zeros((LANES, GLA_KW), F32).at[0:GLA_GATE_RANK].set(gla_w_gate2[l])
        qkv, y_pg = _inproj_mix(h, norm1_g[l].reshape(1, D_MODEL), _prep_w_in(w_in[l]), wg2,
                                gla_b_gate[l].reshape(1, GLA_KW),
                                gla_norm_g[l].reshape(1, GLA_WIDTH), _prep_pool_w(pool_w[l]),
                                pool_scale[l].reshape(1, POOL_WIDTH), batch, seq, ts)
        lam = _diff_lambda(diff_lq1[l], diff_lk1[l], diff_lq2[l], diff_lk2[l], lambda_init)
        y_diff = _diff_attention(qkv, lam, diff_norm_g[l].reshape(1, DIFF_WIDTH),
                                 batch, seq, tq, lambda_init)
        h = _out_mlp(h, y_pg, y_diff, _prep_w_out(w_out[l]), norm2_g[l].reshape(1, D_MODEL),
                     w_mlp1[l].astype(BF16), w_mlp2[l].astype(BF16), gf, tm,
                     final=(l == depth - 1))
    return h.reshape(batch, seq, D_MODEL)
```

```python
import functools
import math

import jax
import jax.numpy as jnp
from jax import lax
from jax.experimental import pallas as pl
from jax.experimental.pallas import tpu as pltpu

F32 = jnp.float32
BF16 = jnp.bfloat16

D_MODEL = 1024
CHUNK = 64
POOL_WIDTH = 256
POOL_WINDOWS = (2, 4, 8, 16)
POOL_GDIM = 64
POOL_HALO = 16
DIFF_HEADS = 4
DIFF_VDIM = 128
DIFF_QKDIM = 64
DIFF_WIDTH = 512
GLA_HEADS = 4
GLA_VDIM = 64
GLA_KDIM = 32
GLA_WIDTH = 256
GLA_KW = GLA_HEADS * GLA_KDIM
GLA_GATE_RANK = 16
GLA_GATE_TAU = 16.0
D_FF = 4096
EPS = 1e-6

LANES = 128
QKV_W = 3 * DIFF_WIDTH
REST_POOL = 0
REST_GQ = 256
REST_GK = 384
REST_GV = 512
REST_GR = 768
REST_GG = 1024
REST_W = 1152
W_IN_COLS = QKV_W + REST_W

VMEM_LIMIT = 56 * 1024 * 1024
NEG_BIG = -1e30
LOG2_E = 1.4426950408889634
ATTN_ROW_BLOCK = 256
ATTN_PAIRS_PER_TRIP = 8
MIX_BATCH_ROWS = 4
QKV_TILE = 256


def _resident(shape):
    nd = len(shape)
    return pl.BlockSpec(shape, lambda *_: (0,) * nd, pipeline_mode=pl.Buffered(1))


def _split_bf16(a):
    hi = a.astype(BF16)
    lo = (a - hi.astype(F32)).astype(BF16)
    return hi, lo


def _attn_kernel(q_ref, k_ref, v_ref, lam_ref, g_ref, o_ref,
                 qs_scr, e_scr, ka_scr, va_scr, sa_scr, sb_scr, la_scr, lb_scr, m_scr, acc_scr,
                 *, tq, rb, nq, lambda_init):
    m_rows = 2 * tq
    nblk = m_rows // rb
    reps = tq // LANES
    nitems = nq * (nq + 1) // 2

    va_scr[:, 0:DIFF_VDIM] = v_ref[...]
    va_scr[:, DIFF_VDIM:2 * DIFF_VDIM] = jnp.ones((va_scr.shape[0], DIFF_VDIM), BF16)

    lane = lax.broadcasted_iota(jnp.int32, (tq, LANES), 1)

    def stack_q(i, carry):
        q = q_ref[pl.ds(pl.multiple_of(i * tq, tq), tq), :]
        zero = jnp.zeros_like(q)
        qs_scr[i, 0:tq, :] = jnp.where(lane < DIFF_QKDIM, q, zero)
        qs_scr[i, tq:m_rows, :] = jnp.where(lane >= DIFF_QKDIM, q, zero)
        return carry

    lax.fori_loop(0, nq, stack_q, 0, unroll=math.gcd(nq, 4))

    seq = k_ref.shape[0]
    ka_scr[:, 0:LANES] = k_ref[...]
    kchunk = (lax.broadcasted_iota(jnp.int32, (seq, LANES), 0) % tq) // CHUNK
    klane = lax.broadcasted_iota(jnp.int32, (seq, LANES), 1)
    ka_scr[:, LANES:2 * LANES] = jnp.where(kchunk > klane, 1.0, 0.0).astype(BF16)
    qchunk = (lax.broadcasted_iota(jnp.int32, (m_rows, LANES), 0) % tq) // CHUNK
    qlane = lax.broadcasted_iota(jnp.int32, (m_rows, LANES), 1)
    e_scr[0] = jnp.zeros((m_rows, LANES), BF16)
    e_scr[1] = jnp.where(qchunk == qlane, NEG_BIG, 0.0).astype(BF16)
    acc_scr[...] = jnp.zeros(acc_scr.shape, F32)

    def qk_block(item, s_dst, r):
        qi, j = item
        rows = slice(r * rb, (r + 1) * rb)
        k = ka_scr[pl.ds(pl.multiple_of(j * tq, tq), tq), :]
        q = jnp.concatenate([qs_scr[qi, rows, :], e_scr[(j == qi).astype(jnp.int32), rows, :]],
                            axis=1)
        s = lax.dot_general(q, k, (((1,), (1,)), ((), ())), preferred_element_type=F32)
        s_dst[0][rows, :] = s
        lane_max = s[:, 0:LANES]
        for c in range(1, reps):
            lane_max = jnp.maximum(lane_max, s[:, c * LANES:(c + 1) * LANES])
        s_dst[1][rows, :] = lane_max

    def softmax_pv_block(item, s_src, r):
        qi, j = item
        rows = slice(r * rb, (r + 1) * rb)
        m_prev = jnp.where(j == 0, -jnp.inf, m_scr[rows, :])
        m_next = jnp.maximum(m_prev, jnp.max(s_src[1][rows, :], axis=1, keepdims=True))
        alpha = jnp.exp2(m_prev - m_next)
        p = jnp.exp2(s_src[0][rows, :] - jnp.concatenate([m_next] * reps, axis=1))
        m_scr[rows, :] = m_next
        va = va_scr[pl.ds(pl.multiple_of(j * tq, tq), tq), :]
        acc_scr[qi, rows, :] = (jnp.concatenate([alpha, alpha], axis=1) * acc_scr[qi, rows, :]
                                + jnp.dot(p.astype(BF16), va, preferred_element_type=F32))

    def stage(item_next, s_next, item_cur, s_cur):
        for r in range(nblk):
            if item_next is not None:
                qk_block(item_next, s_next, r)
            softmax_pv_block(item_cur, s_cur, r)

    def following(item):
        qi, j = item
        last = j == qi
        qn = jnp.where(last, qi + 1, qi)
        jn = jnp.where(last, 0, j + 1)
        over = qn >= nq
        return jnp.where(over, qi, qn), jnp.where(over, j, jn)

    first = (jnp.int32(0), jnp.int32(0))
    buf_a = (sa_scr, la_scr)
    buf_b = (sb_scr, lb_scr)
    for r in range(nblk):
        qk_block(first, buf_a, r)

    def pair(item0):
        item1 = following(item0)
        item2 = following(item1)
        stage(item1, buf_b, item0, buf_a)
        stage(item2, buf_a, item1, buf_b)
        return item2

    def body(_, item):
        for _ in range(ATTN_PAIRS_PER_TRIP):
            item = pair(item)
        return item

    npairs = nitems // 2
    item = lax.fori_loop(0, npairs // ATTN_PAIRS_PER_TRIP, body, first)
    for _ in range(npairs % ATTN_PAIRS_PER_TRIP):
        item = pair(item)
    if nitems % 2:
        stage(None, None, item, buf_a)

    def finalize(i, carry):
        acc = acc_scr[i]
        out = acc[:, 0:DIFF_VDIM] / acc[:, DIFF_VDIM:2 * DIFF_VDIM]
        o = out[0:tq] - lam_ref[...] * out[tq:m_rows]
        ms = jnp.mean(o * o, axis=-1, keepdims=True)
        o = o * lax.rsqrt(ms + EPS) * g_ref[...] * (1.0 - lambda_init)
        o_ref[pl.ds(pl.multiple_of(i * tq, tq), tq), :] = o.astype(o_ref.dtype)
        return carry

    lax.fori_loop(0, nq, finalize, 0, unroll=math.gcd(nq, 4))


def _lambda_kernel(lq1_ref, lk1_ref, lq2_ref, lk2_ref, lam_ref, *, lambda_init):
    a = jnp.sum(lq1_ref[...] * lk1_ref[...], axis=-1, keepdims=True)
    b = jnp.sum(lq2_ref[...] * lk2_ref[...], axis=-1, keepdims=True)
    lam_ref[...] = jnp.broadcast_to(jnp.exp(a) - jnp.exp(b) + lambda_init, lam_ref.shape)


def _diff_lambda(lq1, lk1, lq2, lk2, lambda_init):
    args = [a.reshape(1, DIFF_QKDIM) for a in (lq1, lk1, lq2, lk2)]
    return pl.pallas_call(
        functools.partial(_lambda_kernel, lambda_init=lambda_init),
        out_shape=jax.ShapeDtypeStruct((1, LANES), F32),
        name="diff_lambda",
    )(*args)


def _diff_attention(qkv, lam, norm_g, batch, seq, tq, lambda_init):
    t = batch * seq
    nq = seq // tq
    kcol = DIFF_WIDTH // DIFF_VDIM
    vcol = 2 * DIFF_WIDTH // DIFF_VDIM
    return pl.pallas_call(
        functools.partial(_attn_kernel, tq=tq, rb=min(ATTN_ROW_BLOCK, 2 * tq), nq=nq,
                          lambda_init=lambda_init),
        grid=(batch, DIFF_HEADS),
        in_specs=[
            pl.BlockSpec((seq, DIFF_VDIM), lambda b, h: (b, h)),
            pl.BlockSpec((seq, DIFF_VDIM), lambda b, h: (b, kcol + h)),
            pl.BlockSpec((seq, DIFF_VDIM), lambda b, h: (b, vcol + h)),
            pl.BlockSpec((1, LANES), lambda b, h: (0, 0)),
            pl.BlockSpec((1, DIFF_VDIM), lambda b, h: (0, h)),
        ],
        out_specs=pl.BlockSpec((seq, DIFF_VDIM), lambda b, h: (b, h)),
        out_shape=jax.ShapeDtypeStruct((t, DIFF_WIDTH), BF16),
        scratch_shapes=[
            pltpu.VMEM((nq, 2 * tq, LANES), BF16),
            pltpu.VMEM((2, 2 * tq, LANES), BF16),
            pltpu.VMEM((seq, 2 * LANES), BF16),
            pltpu.VMEM((seq, 2 * DIFF_VDIM), BF16),
            pltpu.VMEM((2 * tq, tq), F32),
            pltpu.VMEM((2 * tq, tq), F32),
            pltpu.VMEM((2 * tq, LANES), F32),
            pltpu.VMEM((2 * tq, LANES), F32),
            pltpu.VMEM((2 * tq, LANES), F32),
            pltpu.VMEM((nq, 2 * tq, 2 * DIFF_VDIM), F32),
        ],
        compiler_params=pltpu.CompilerParams(
            dimension_semantics=("parallel", "parallel"), vmem_limit_bytes=VMEM_LIMIT),
        name="diff_attention",
    )(qkv, qkv, qkv, lam, norm_g)


def _inproj_mix_kernel(h_ref, g_ref, w_ref, wg2_ref, bg_ref, gng_ref, pw_ref, ps_ref,
                       qkv_ref, out_ref, state_scr, halo_scr, *, ts, nb):
    st = pl.program_id(1)
    nchunk = ts // CHUNK
    rows = range(nb)

    @pl.when(st == 0)
    def _():
        state_scr[...] = jnp.zeros_like(state_scr)
        halo_scr[:, 0:POOL_HALO, :] = jnp.zeros((nb, POOL_HALO, POOL_WIDTH), F32)

    def dot(a, b):
        return jnp.dot(a, b, preferred_element_type=F32)

    def dot_nt(a, b):
        return lax.dot_general(a, b, (((1,), (1,)), ((), ())), preferred_element_type=F32)

    def dot_tn(a, b):
        return lax.dot_general(a, b, (((0,), (0,)), ((), ())), preferred_element_type=F32)

    h = h_ref[...].reshape(nb * ts, D_MODEL)
    ms = jnp.mean(h * h, axis=-1, keepdims=True)
    u = (h * lax.rsqrt(ms + EPS) * g_ref[...]).astype(BF16)
    rest_all = dot(u, w_ref[:, QKV_W:])
    rest = [rest_all[b * ts:(b + 1) * ts] for b in rows]

    qkv_tiles = iter(range(QKV_W // QKV_TILE))

    def emit_qkv(count=1):
        for _ in range(count):
            c = next(qkv_tiles, None)
            if c is not None:
                cols = slice(c * QKV_TILE, (c + 1) * QKV_TILE)
                y = dot(u, w_ref[:, cols]).astype(BF16)
                for b in rows:
                    qkv_ref[b, :, cols] = y[b * ts:(b + 1) * ts]

    x = [rest[b][:, REST_POOL:REST_POOL + POOL_WIDTH] for b in rows]
    for b in rows:
        halo_scr[b, POOL_HALO:POOL_HALO + ts, :] = x[b]

    def window_sums(b):
        xf = halo_scr[b]
        xa = xf[:, 0:LANES]
        c2 = xa + pltpu.roll(xa, 1, axis=0)
        c4 = c2 + pltpu.roll(c2, 2, axis=0)
        xb = xf[:, LANES:2 * LANES]
        d2 = xb + pltpu.roll(xb, 1, axis=0)
        d4 = d2 + pltpu.roll(d2, 2, axis=0)
        c8 = d4 + pltpu.roll(d4, 4, axis=0)
        c16 = c8 + pltpu.roll(c8, 8, axis=0)
        cut = slice(POOL_HALO, POOL_HALO + ts)
        return xa[cut], xb[cut], c2[cut], c4[cut], c8[cut], c16[cut]

    sums = [window_sums(b) for b in rows]
    tpos = (st * ts + lax.broadcasted_iota(jnp.int32, (ts, LANES), 0) + 1).astype(F32)
    lane = lax.broadcasted_iota(jnp.int32, (ts, LANES), 1)
    first = lane < POOL_GDIM
    inv_a = jnp.where(first, 1.0 / jnp.minimum(tpos, 2.0), 1.0 / jnp.minimum(tpos, 4.0))
    inv_b = jnp.where(first, 1.0 / jnp.minimum(tpos, 8.0), 1.0 / jnp.minimum(tpos, 16.0))

    def pooled_of(xa, xb, c2, c4, c8, c16):
        mean_a = jnp.where(first, c2, c4) * inv_a
        mean_b = jnp.where(first, c8, c16) * inv_b
        return jnp.concatenate([mean_a - xa, mean_b - xb], axis=1).astype(BF16)

    pooled = [pooled_of(*s) for s in sums]
    pw = pw_ref[...]
    y_pool = [dot(p, pw) * ps_ref[...] for p in pooled]

    emit_qkv()
    gq = [rest[b][:, REST_GQ:REST_GQ + GLA_KW] for b in rows]
    gk = [rest[b][:, REST_GK:REST_GK + GLA_KW] for b in rows]
    gv_b = [rest[b][:, REST_GV:REST_GV + GLA_WIDTH].astype(BF16) for b in rows]
    gr = [rest[b][:, REST_GR:REST_GR + GLA_WIDTH] for b in rows]
    gg = [rest[b][:, REST_GG:REST_GG + LANES] for b in rows]

    w_hi, w_lo = _split_bf16(wg2_ref[...])
    w_cat = jnp.concatenate([w_hi, w_lo], axis=1)
    g_split = [_split_bf16(v) for v in gg]
    xg = []
    for hi, lo in g_split:
        hh = dot(hi, w_cat)
        xg.append(hh[:, 0:GLA_KW] + hh[:, GLA_KW:2 * GLA_KW] + dot(lo, w_hi) + bg_ref[...])
    log_a = [(jnp.minimum(v, 0.0) - jnp.log(1.0 + jnp.exp(-jnp.abs(v)))) * (1.0 / GLA_GATE_TAU)
             for v in xg]

    emit_qkv()
    r = lax.broadcasted_iota(jnp.int32, (ts, ts), 0)
    c = lax.broadcasted_iota(jnp.int32, (ts, ts), 1)
    same_chunk = (r // CHUNK) == (c // CHUNK)
    causal = jnp.logical_and(same_chunk, c <= r)
    tri = jnp.where(causal, 1.0, 0.0).astype(BF16)
    blk = jnp.where(same_chunk, 1.0, 0.0).astype(BF16)
    la_cat = [jnp.concatenate(_split_bf16(v), axis=1) for v in log_a]
    bcum2 = [dot(tri, v) for v in la_cat]
    btot2 = [dot(blk, v) for v in la_cat]
    bcum = [v[:, 0:GLA_KW] + v[:, GLA_KW:2 * GLA_KW] for v in bcum2]
    btot = [v[:, 0:GLA_KW] + v[:, GLA_KW:2 * GLA_KW] for v in btot2]

    q_dec_b = [((q * (GLA_KDIM ** -0.5)) * jnp.exp(bc)).astype(BF16) for q, bc in zip(gq, bcum)]
    k_inv = [(k * jnp.exp(-bc)).astype(BF16) for k, bc in zip(gk, bcum)]
    k_end = [(k * jnp.exp(bt - bc)).astype(BF16) for k, bt, bc in zip(gk, btot, bcum)]
    decay = [jnp.exp(bt) for bt in btot]

    emit_qkv()
    klane = lax.broadcasted_iota(jnp.int32, (ts, GLA_KW), 1)
    vlane = lax.broadcasted_iota(jnp.int32, (ts, GLA_WIDTH), 1)
    o = [None] * nb
    for h in range(GLA_HEADS):
        qh = [jnp.where(klane // GLA_KDIM == h, q, jnp.zeros_like(q)) for q in q_dec_b]
        att = [jnp.where(causal, dot_nt(q, k), 0.0).astype(BF16) for q, k in zip(qh, k_inv)]
        vh = [jnp.where(vlane // GLA_VDIM == h, v, jnp.zeros_like(v)) for v in gv_b]
        part = [dot(a, v) for a, v in zip(att, vh)]
        o = part if h == 0 else [a + b for a, b in zip(o, part)]
        if h % 2 == 1:
            emit_qkv()

    srow = lax.broadcasted_iota(jnp.int32, (GLA_WIDTH, GLA_KW), 0)
    scol = lax.broadcasted_iota(jnp.int32, (GLA_WIDTH, GLA_KW), 1)
    head_diag = (srow // GLA_VDIM) == (scol // GLA_KDIM)
    state = [state_scr[b] for b in rows]
    o_inter = [[] for _ in rows]
    for n in range(nchunk):
        sl = slice(n * CHUNK, (n + 1) * CHUNK)
        for b in rows:
            o_inter[b].append(dot_nt(q_dec_b[b][sl], state[b].astype(BF16)))
        kv_t = [dot_tn(gv_b[b][sl], k_end[b][sl]) for b in rows]
        state = [state[b] * decay[b][n * CHUNK:n * CHUNK + 1, :] + jnp.where(head_diag, kv_t[b], 0.0)
                 for b in rows]
    o = [o[b] + jnp.concatenate(o_inter[b], axis=0) for b in rows]

    emit_qkv()
    hr = lax.broadcasted_iota(jnp.int32, (GLA_WIDTH, GLA_WIDTH), 0)
    hc = lax.broadcasted_iota(jnp.int32, (GLA_WIDTH, GLA_WIDTH), 1)
    ones_blk = jnp.where((hr // GLA_VDIM) == (hc // GLA_VDIM), 1.0, 0.0).astype(BF16)
    sq_split = [_split_bf16(v * v) for v in o]
    ms = [(dot(hi, ones_blk) + dot(lo, ones_blk)) * (1.0 / GLA_VDIM) for hi, lo in sq_split]
    o = [v * lax.rsqrt(m + EPS) * gng_ref[...] for v, m in zip(o, ms)]
    silu = [g / (1.0 + jnp.exp(-g)) for g in gr]

    emit_qkv(QKV_W // QKV_TILE)
    for b in rows:
        out_ref[b] = jnp.concatenate([y_pool[b], o[b] * silu[b]], axis=1).astype(out_ref.dtype)
        state_scr[b] = state[b]
        halo_scr[b, 0:POOL_HALO, :] = x[b][ts - POOL_HALO:ts, :]


def _inproj_mix(h, g, w, wg2, bg, gng, pw, ps, batch, seq, ts):
    t = batch * seq
    nb = math.gcd(batch, MIX_BATCH_ROWS)
    qkv, y_pg = pl.pallas_call(
        functools.partial(_inproj_mix_kernel, ts=ts, nb=nb),
        grid=(batch // nb, seq // ts),
        in_specs=[
            pl.BlockSpec((nb, ts, D_MODEL), lambda b, s: (b, s, 0)),
            _resident((1, D_MODEL)),
            _resident((D_MODEL, W_IN_COLS)),
            _resident((LANES, GLA_KW)),
            _resident((1, GLA_KW)),
            _resident((1, GLA_WIDTH)),
            _resident((POOL_WIDTH, POOL_WIDTH)),
            _resident((1, POOL_WIDTH)),
        ],
        out_specs=[
            pl.BlockSpec((nb, ts, QKV_W), lambda b, s: (b, s, 0)),
            pl.BlockSpec((nb, ts, POOL_WIDTH + GLA_WIDTH), lambda b, s: (b, s, 0)),
        ],
        out_shape=[
            jax.ShapeDtypeStruct((batch, seq, QKV_W), BF16),
            jax.ShapeDtypeStruct((batch, seq, POOL_WIDTH + GLA_WIDTH), BF16),
        ],
        scratch_shapes=[
            pltpu.VMEM((nb, GLA_WIDTH, GLA_KW), F32),
            pltpu.VMEM((nb, POOL_HALO + ts, POOL_WIDTH), F32),
        ],
        compiler_params=pltpu.CompilerParams(
            dimension_semantics=("parallel", "arbitrary"), vmem_limit_bytes=VMEM_LIMIT),
        name="inproj_mix",
    )(h.reshape(batch, seq, D_MODEL), g, w, wg2, bg, gng, pw, ps)
    return qkv.reshape(t, QKV_W), y_pg.reshape(t, POOL_WIDTH + GLA_WIDTH)


def _out_mlp_kernel(h_ref, ypg_ref, yd_ref, wo_ref, g2_ref, w1_ref, w2_ref, gf_ref, o_ref,
                    *, final):
    npg = POOL_WIDTH + GLA_WIDTH
    h1 = (h_ref[...]
          + jnp.dot(ypg_ref[...], wo_ref[0:npg, :], preferred_element_type=F32)
          + jnp.dot(yd_ref[...], wo_ref[npg:D_MODEL, :], preferred_element_type=F32))
    ms = jnp.mean(h1 * h1, axis=-1, keepdims=True)
    z = (h1 * lax.rsqrt(ms + EPS) * g2_ref[...]).astype(BF16)
    a = jnp.maximum(jnp.dot(z, w1_ref[...], preferred_element_type=F32), 0.0)
    acc = h1 + jnp.dot((a * a).astype(BF16), w2_ref[...], preferred_element_type=F32)
    if final:
        ms = jnp.mean(acc * acc, axis=-1, keepdims=True)
        acc = acc * lax.rsqrt(ms + EPS) * gf_ref[...]
    o_ref[...] = acc


def _out_mlp(h, ypg, yd, wo, g2, w1, w2, gf, tm, final):
    t = h.shape[0]
    return pl.pallas_call(
        functools.partial(_out_mlp_kernel, final=final),
        grid=(t // tm,),
        in_specs=[
            pl.BlockSpec((tm, D_MODEL), lambda i: (i, 0)),
            pl.BlockSpec((tm, POOL_WIDTH + GLA_WIDTH), lambda i: (i, 0)),
            pl.BlockSpec((tm, DIFF_WIDTH), lambda i: (i, 0)),
            _resident((D_MODEL, D_MODEL)),
            _resident((1, D_MODEL)),
            _resident((D_MODEL, D_FF)),
            _resident((D_FF, D_MODEL)),
            _resident((1, D_MODEL)),
        ],
        out_specs=pl.BlockSpec((tm, D_MODEL), lambda i: (i, 0)),
        out_shape=jax.ShapeDtypeStruct((t, D_MODEL), F32),
        compiler_params=pltpu.CompilerParams(
            dimension_semantics=("parallel",), vmem_limit_bytes=VMEM_LIMIT),
        name="out_mlp",
    )(h, ypg, yd, wo, g2, w1, w2, gf)


def _prep_w_in(w):
    pool = w[:, 0:256]
    dq = w[:, 256:768] * (DIFF_QKDIM ** -0.5 * LOG2_E)
    dk = w[:, 768:1280]
    dv = w[:, 1280:1792]
    gq = w[:, 1792:1920]
    gk = w[:, 1920:2048]
    gv = w[:, 2048:2304]
    gr = w[:, 2304:2560]
    gg = w[:, 2560:2576]
    pad = jnp.zeros((D_MODEL, REST_W - REST_GG - GLA_GATE_RANK), w.dtype)
    return jnp.concatenate([dq, dk, dv, pool, gq, gk, gv, gr, gg, pad], axis=1).astype(BF16)


def _prep_w_out(w):
    return jnp.concatenate([w[0:256], w[768:1024], w[256:768]], axis=0).astype(BF16)


def _prep_pool_w(pw):
    out = jnp.zeros((POOL_WIDTH, POOL_WIDTH), F32)
    for g in range(len(POOL_WINDOWS)):
        out = out.at[g * POOL_GDIM:(g + 1) * POOL_GDIM, g * POOL_GDIM:(g + 1) * POOL_GDIM].set(pw[g])
    return out.astype(BF16)


def _tiles(seq, t):
    tm = math.gcd(t, 512)
    tq = math.gcd(seq, 512)
    ts = math.gcd(seq, 256)
    return tm, tq, ts


def kernel(x, norm1_g, w_in, pool_w, pool_scale, diff_lq1, diff_lk1, diff_lq2, diff_lk2, diff_norm_g, gla_w_gate2, gla_b_gate, gla_norm_g, w_out, norm2_g, w_mlp1, w_mlp2, final_norm_g):
    batch, seq, d = x.shape
    assert d == D_MODEL and seq % CHUNK == 0
    depth = w_in.shape[0]
    t = batch * seq
    tm, tq, ts = _tiles(seq, t)
    h = x.reshape(t, D_MODEL)
    gf = final_norm_g.reshape(1, D_MODEL)
    for l in range(depth):
        lambda_init = 0.8 - 0.6 * math.exp(-0.3 * l)
        wg2 = jnp.zeros((LANES, GLA_KW), F32).at[0:GLA_GATE_RANK].set(gla_w_gate2[l])
        qkv, y_pg = _inproj_mix(h, norm1_g[l].reshape(1, D_MODEL), _prep_w_in(w_in[l]), wg2,
                                gla_b_gate[l].reshape(1, GLA_KW),
                                gla_norm_g[l].reshape(1, GLA_WIDTH), _prep_pool_w(pool_w[l]),
                                pool_scale[l].reshape(1, POOL_WIDTH), batch, seq, ts)
        lam = _diff_lambda(diff_lq1[l], diff_lk1[l], diff_lq2[l], diff_lk2[l], lambda_init)
        y_diff = _diff_attention(qkv, lam, diff_norm_g[l].reshape(1, DIFF_WIDTH),
                                 batch, seq, tq, lambda_init)
        h = _out_mlp(h, y_pg, y_diff, _prep_w_out(w_out[l]), norm2_g[l].reshape(1, D_MODEL),
                     w_mlp1[l].astype(BF16), w_mlp2[l].astype(BF16), gf, tm,
                     final=(l == depth - 1))
    return h.reshape(batch, seq, D_MODEL)
```

```python
import functools
import math

import jax
import jax.numpy as jnp
from jax import lax
from jax.experimental import pallas as pl
from jax.experimental.pallas import tpu as pltpu

F32 = jnp.float32
BF16 = jnp.bfloat16

D_MODEL = 1024
CHUNK = 64
POOL_WIDTH = 256
POOL_WINDOWS = (2, 4, 8, 16)
POOL_GDIM = 64
POOL_HALO = 16
DIFF_HEADS = 4
DIFF_VDIM = 128
DIFF_QKDIM = 64
DIFF_WIDTH = 512
GLA_HEADS = 4
GLA_VDIM = 64
GLA_KDIM = 32
GLA_WIDTH = 256
GLA_KW = GLA_HEADS * GLA_KDIM
GLA_GATE_RANK = 16
GLA_GATE_TAU = 16.0
D_FF = 4096
EPS = 1e-6

LANES = 128
QKV_W = 3 * DIFF_WIDTH
REST_POOL = 0
REST_GQ = 256
REST_GK = 384
REST_GV = 512
REST_GR = 768
REST_GG = 1024
REST_W = 1152
W_IN_COLS = QKV_W + REST_W

VMEM_LIMIT = 56 * 1024 * 1024
NEG_BIG = -1e30
LOG2_E = 1.4426950408889634
ATTN_ROW_BLOCK = 256
ATTN_PAIRS_PER_TRIP = 8
MIX_BATCH_ROWS = 4
QKV_TILE = 256


def _resident(shape):
    nd = len(shape)
    return pl.BlockSpec(shape, lambda *_: (0,) * nd, pipeline_mode=pl.Buffered(1))


def _split_bf16(a):
    hi = a.astype(BF16)
    lo = (a - hi.astype(F32)).astype(BF16)
    return hi, lo


def _attn_kernel(q_ref, k_ref, v_ref, lam_ref, g_ref, o_ref,
                 qs_scr, e_scr, ka_scr, va_scr, sa_scr, sb_scr, la_scr, lb_scr, m_scr, acc_scr,
                 *, tq, rb, nq, lambda_init):
    m_rows = 2 * tq
    nblk = m_rows // rb
    reps = tq // LANES
    nitems = nq * (nq + 1) // 2

    seq = k_ref.shape[0]

    @pl.when(jnp.logical_and(pl.program_id(0) == 0, pl.program_id(1) == 0))
    def _():
        va_scr[:, DIFF_VDIM:2 * DIFF_VDIM] = jnp.ones((seq, DIFF_VDIM), BF16)
        kchunk = (lax.broadcasted_iota(jnp.int32, (seq, LANES), 0) % tq) // CHUNK
        klane = lax.broadcasted_iota(jnp.int32, (seq, LANES), 1)
        ka_scr[:, LANES:2 * LANES] = jnp.where(kchunk > klane, 1.0, 0.0).astype(BF16)
        qchunk = (lax.broadcasted_iota(jnp.int32, (m_rows, LANES), 0) % tq) // CHUNK
        qlane = lax.broadcasted_iota(jnp.int32, (m_rows, LANES), 1)
        e_scr[0] = jnp.zeros((m_rows, LANES), BF16)
        e_scr[1] = jnp.where(qchunk == qlane, NEG_BIG, 0.0).astype(BF16)

    va_scr[:, 0:DIFF_VDIM] = v_ref[...]
    ka_scr[:, 0:LANES] = k_ref[...]
    acc_scr[...] = jnp.zeros(acc_scr.shape, F32)

    lane = lax.broadcasted_iota(jnp.int32, (tq, LANES), 1)

    def stack_q(i, carry):
        q = q_ref[pl.ds(pl.multiple_of(i * tq, tq), tq), :]
        zero = jnp.zeros_like(q)
        qs_scr[i, 0:tq, :] = jnp.where(lane < DIFF_QKDIM, q, zero)
        qs_scr[i, tq:m_rows, :] = jnp.where(lane >= DIFF_QKDIM, q, zero)
        return carry

    lax.fori_loop(0, nq, stack_q, 0, unroll=math.gcd(nq, 4))

    def qk_block(item, s_dst, r):
        qi, j = item
        rows = slice(r * rb, (r + 1) * rb)
        k = ka_scr[pl.ds(pl.multiple_of(j * tq, tq), tq), :]
        q = jnp.concatenate([qs_scr[qi, rows, :], e_scr[(j == qi).astype(jnp.int32), rows, :]],
                            axis=1)
        s = lax.dot_general(q, k, (((1,), (1,)), ((), ())), preferred_element_type=F32)
        s_dst[0][rows, :] = s
        lane_max = s[:, 0:LANES]
        for c in range(1, reps):
            lane_max = jnp.maximum(lane_max, s[:, c * LANES:(c + 1) * LANES])
        s_dst[1][rows, :] = lane_max

    def softmax_pv_block(item, s_src, r):
        qi, j = item
        rows = slice(r * rb, (r + 1) * rb)
        m_prev = jnp.where(j == 0, -jnp.inf, m_scr[rows, :])
        m_next = jnp.maximum(m_prev, jnp.max(s_src[1][rows, :], axis=1, keepdims=True))
        alpha = jnp.exp2(m_prev - m_next)
        p = jnp.exp2(s_src[0][rows, :] - jnp.concatenate([m_next] * reps, axis=1))
        m_scr[rows, :] = m_next
        va = va_scr[pl.ds(pl.multiple_of(j * tq, tq), tq), :]
        acc_scr[qi, rows, :] = (jnp.concatenate([alpha, alpha], axis=1) * acc_scr[qi, rows, :]
                                + jnp.dot(p.astype(BF16), va, preferred_element_type=F32))

    def stage(item_next, s_next, item_cur, s_cur):
        for r in range(nblk):
            if item_next is not None:
                qk_block(item_next, s_next, r)
            softmax_pv_block(item_cur, s_cur, r)

    def following(item):
        qi, j = item
        last = j == qi
        qn = jnp.where(last, qi + 1, qi)
        jn = jnp.where(last, 0, j + 1)
        over = qn >= nq
        return jnp.where(over, qi, qn), jnp.where(over, j, jn)

    first = (jnp.int32(0), jnp.int32(0))
    buf_a = (sa_scr, la_scr)
    buf_b = (sb_scr, lb_scr)
    for r in range(nblk):
        qk_block(first, buf_a, r)

    def pair(item0):
        item1 = following(item0)
        item2 = following(item1)
        stage(item1, buf_b, item0, buf_a)
        stage(item2, buf_a, item1, buf_b)
        return item2

    def body(_, item):
        for _ in range(ATTN_PAIRS_PER_TRIP):
            item = pair(item)
        return item

    npairs = nitems // 2
    item = lax.fori_loop(0, npairs // ATTN_PAIRS_PER_TRIP, body, first)
    for _ in range(npairs % ATTN_PAIRS_PER_TRIP):
        item = pair(item)
    if nitems % 2:
        stage(None, None, item, buf_a)

    def finalize(i, carry):
        acc = acc_scr[i]
        a1, l1 = acc[0:tq, 0:DIFF_VDIM], acc[0:tq, DIFF_VDIM:2 * DIFF_VDIM]
        a2, l2 = acc[tq:m_rows, 0:DIFF_VDIM], acc[tq:m_rows, DIFF_VDIM:2 * DIFF_VDIM]
        o = (a1 * l2 - lam_ref[...] * (a2 * l1)) / (l1 * l2)
        ms = jnp.mean(o * o, axis=-1, keepdims=True)
        o = o * lax.rsqrt(ms + EPS) * g_ref[...] * (1.0 - lambda_init)
        o_ref[pl.ds(pl.multiple_of(i * tq, tq), tq), :] = o.astype(o_ref.dtype)
        return carry

    lax.fori_loop(0, nq, finalize, 0, unroll=math.gcd(nq, 4))


def _lambda_kernel(lq1_ref, lk1_ref, lq2_ref, lk2_ref, lam_ref, *, lambda_init):
    a = jnp.sum(lq1_ref[...] * lk1_ref[...], axis=-1, keepdims=True)
    b = jnp.sum(lq2_ref[...] * lk2_ref[...], axis=-1, keepdims=True)
    lam_ref[...] = jnp.broadcast_to(jnp.exp(a) - jnp.exp(b) + lambda_init, lam_ref.shape)


def _diff_lambda(lq1, lk1, lq2, lk2, lambda_init):
    args = [a.reshape(1, DIFF_QKDIM) for a in (lq1, lk1, lq2, lk2)]
    return pl.pallas_call(
        functools.partial(_lambda_kernel, lambda_init=lambda_init),
        out_shape=jax.ShapeDtypeStruct((1, LANES), F32),
        name="diff_lambda",
    )(*args)


def _diff_attention(qkv, lam, norm_g, batch, seq, tq, lambda_init):
    t = batch * seq
    nq = seq // tq
    kcol = DIFF_WIDTH // DIFF_VDIM
    vcol = 2 * DIFF_WIDTH // DIFF_VDIM
    return pl.pallas_call(
        functools.partial(_attn_kernel, tq=tq, rb=min(ATTN_ROW_BLOCK, 2 * tq), nq=nq,
                          lambda_init=lambda_init),
        grid=(batch, DIFF_HEADS),
        in_specs=[
            pl.BlockSpec((seq, DIFF_VDIM), lambda b, h: (b, h)),
            pl.BlockSpec((seq, DIFF_VDIM), lambda b, h: (b, kcol + h)),
            pl.BlockSpec((seq, DIFF_VDIM), lambda b, h: (b, vcol + h)),
            pl.BlockSpec((1, LANES), lambda b, h: (0, 0)),
            pl.BlockSpec((1, DIFF_VDIM), lambda b, h: (0, h)),
        ],
        out_specs=pl.BlockSpec((seq, DIFF_VDIM), lambda b, h: (b, h)),
        out_shape=jax.ShapeDtypeStruct((t, DIFF_WIDTH), BF16),
        scratch_shapes=[
            pltpu.VMEM((nq, 2 * tq, LANES), BF16),
            pltpu.VMEM((2, 2 * tq, LANES), BF16),
            pltpu.VMEM((seq, 2 * LANES), BF16),
            pltpu.VMEM((seq, 2 * DIFF_VDIM), BF16),
            pltpu.VMEM((2 * tq, tq), F32),
            pltpu.VMEM((2 * tq, tq), F32),
            pltpu.VMEM((2 * tq, LANES), F32),
            pltpu.VMEM((2 * tq, LANES), F32),
            pltpu.VMEM((2 * tq, LANES), F32),
            pltpu.VMEM((nq, 2 * tq, 2 * DIFF_VDIM), F32),
        ],
        compiler_params=pltpu.CompilerParams(
            dimension_semantics=("arbitrary", "arbitrary"), vmem_limit_bytes=VMEM_LIMIT),
        name="diff_attention",
    )(qkv, qkv, qkv, lam, norm_g)


def _inproj_mix_kernel(h_ref, g_ref, w_ref, wg2_ref, bg_ref, gng_ref, pw_ref, ps_ref,
                       qkv_ref, out_ref, state_scr, halo_scr, *, ts, nb):
    st = pl.program_id(1)
    nchunk = ts // CHUNK
    rows = range(nb)

    @pl.when(st == 0)
    def _():
        state_scr[...] = jnp.zeros_like(state_scr)
        halo_scr[:, 0:POOL_HALO, :] = jnp.zeros((nb, POOL_HALO, POOL_WIDTH), F32)

    def dot(a, b):
        return jnp.dot(a, b, preferred_element_type=F32)

    def dot_nt(a, b):
        return lax.dot_general(a, b, (((1,), (1,)), ((), ())), preferred_element_type=F32)

    def dot_tn(a, b):
        return lax.dot_general(a, b, (((0,), (0,)), ((), ())), preferred_element_type=F32)

    h = h_ref[...].reshape(nb * ts, D_MODEL)
    ms = jnp.mean(h * h, axis=-1, keepdims=True)
    u = (h * lax.rsqrt(ms + EPS) * g_ref[...]).astype(BF16)
    rest_all = dot(u, w_ref[:, QKV_W:])
    rest = [rest_all[b * ts:(b + 1) * ts] for b in rows]

    qkv_tiles = iter(range(QKV_W // QKV_TILE))

    def emit_qkv(count=1):
        for _ in range(count):
            c = next(qkv_tiles, None)
            if c is not None:
                cols = slice(c * QKV_TILE, (c + 1) * QKV_TILE)
                y = dot(u, w_ref[:, cols]).astype(BF16)
                for b in rows:
                    qkv_ref[b, :, cols] = y[b * ts:(b + 1) * ts]

    x = [rest[b][:, REST_POOL:REST_POOL + POOL_WIDTH] for b in rows]
    for b in rows:
        halo_scr[b, POOL_HALO:POOL_HALO + ts, :] = x[b]

    def window_sums(b):
        xf = halo_scr[b]
        xa = xf[:, 0:LANES]
        c2 = xa + pltpu.roll(xa, 1, axis=0)
        c4 = c2 + pltpu.roll(c2, 2, axis=0)
        xb = xf[:, LANES:2 * LANES]
        d2 = xb + pltpu.roll(xb, 1, axis=0)
        d4 = d2 + pltpu.roll(d2, 2, axis=0)
        c8 = d4 + pltpu.roll(d4, 4, axis=0)
        c16 = c8 + pltpu.roll(c8, 8, axis=0)
        cut = slice(POOL_HALO, POOL_HALO + ts)
        return xa[cut], xb[cut], c2[cut], c4[cut], c8[cut], c16[cut]

    sums = [window_sums(b) for b in rows]
    tpos = (st * ts + lax.broadcasted_iota(jnp.int32, (ts, LANES), 0) + 1).astype(F32)
    lane = lax.broadcasted_iota(jnp.int32, (ts, LANES), 1)
    first = lane < POOL_GDIM
    inv_a = jnp.where(first, 1.0 / jnp.minimum(tpos, 2.0), 1.0 / jnp.minimum(tpos, 4.0))
    inv_b = jnp.where(first, 1.0 / jnp.minimum(tpos, 8.0), 1.0 / jnp.minimum(tpos, 16.0))

    def pooled_of(xa, xb, c2, c4, c8, c16):
        mean_a = jnp.where(first, c2, c4) * inv_a
        mean_b = jnp.where(first, c8, c16) * inv_b
        return jnp.concatenate([mean_a - xa, mean_b - xb], axis=1).astype(BF16)

    pooled = [pooled_of(*s) for s in sums]
    pw = pw_ref[...]
    y_pool = [dot(p, pw) * ps_ref[...] for p in pooled]

    emit_qkv()
    gq = [rest[b][:, REST_GQ:REST_GQ + GLA_KW] for b in rows]
    gk = [rest[b][:, REST_GK:REST_GK + GLA_KW] for b in rows]
    gv_b = [rest[b][:, REST_GV:REST_GV + GLA_WIDTH].astype(BF16) for b in rows]
    gr = [rest[b][:, REST_GR:REST_GR + GLA_WIDTH] for b in rows]
    gg = [rest[b][:, REST_GG:REST_GG + LANES] for b in rows]

    w_hi, w_lo = _split_bf16(wg2_ref[...])
    w_cat = jnp.concatenate([w_hi, w_lo], axis=1)
    g_split = [_split_bf16(v) for v in gg]
    xg = []
    for hi, lo in g_split:
        hh = dot(hi, w_cat)
        xg.append(hh[:, 0:GLA_KW] + hh[:, GLA_KW:2 * GLA_KW] + dot(lo, w_hi) + bg_ref[...])
    log_a = [(jnp.minimum(v, 0.0) - jnp.log(1.0 + jnp.exp(-jnp.abs(v)))) * (1.0 / GLA_GATE_TAU)
             for v in xg]

    emit_qkv()
    r = lax.broadcasted_iota(jnp.int32, (ts, ts), 0)
    c = lax.broadcasted_iota(jnp.int32, (ts, ts), 1)
    same_chunk = (r // CHUNK) == (c // CHUNK)
    causal = jnp.logical_and(same_chunk, c <= r)
    tri = jnp.where(causal, 1.0, 0.0).astype(BF16)
    blk = jnp.where(same_chunk, 1.0, 0.0).astype(BF16)
    la_cat = [jnp.concatenate(_split_bf16(v), axis=1) for v in log_a]
    bcum2 = [dot(tri, v) for v in la_cat]
    btot2 = [dot(blk, v) for v in la_cat]
    bcum = [v[:, 0:GLA_KW] + v[:, GLA_KW:2 * GLA_KW] for v in bcum2]
    btot = [v[:, 0:GLA_KW] + v[:, GLA_KW:2 * GLA_KW] for v in btot2]

    q_dec_b = [((q * (GLA_KDIM ** -0.5)) * jnp.exp(bc)).astype(BF16) for q, bc in zip(gq, bcum)]
    k_inv = [(k * jnp.exp(-bc)).astype(BF16) for k, bc in zip(gk, bcum)]
    k_end = [(k * jnp.exp(bt - bc)).astype(BF16) for k, bt, bc in zip(gk, btot, bcum)]
    decay = [jnp.exp(bt) for bt in btot]

    emit_qkv()
    klane = lax.broadcasted_iota(jnp.int32, (ts, GLA_KW), 1)
    vlane = lax.broadcasted_iota(jnp.int32, (ts, GLA_WIDTH), 1)
    o = [None] * nb
    for h in range(GLA_HEADS):
        qh = [jnp.where(klane // GLA_KDIM == h, q, jnp.zeros_like(q)) for q in q_dec_b]
        att = [jnp.where(causal, dot_nt(q, k), 0.0).astype(BF16) for q, k in zip(qh, k_inv)]
        vh = [jnp.where(vlane // GLA_VDIM == h, v, jnp.zeros_like(v)) for v in gv_b]
        part = [dot(a, v) for a, v in zip(att, vh)]
        o = part if h == 0 else [a + b for a, b in zip(o, part)]
        if h % 2 == 1:
            emit_qkv()

    srow = lax.broadcasted_iota(jnp.int32, (GLA_WIDTH, GLA_KW), 0)
    scol = lax.broadcasted_iota(jnp.int32, (GLA_WIDTH, GLA_KW), 1)
    head_diag = (srow // GLA_VDIM) == (scol // GLA_KDIM)
    state = [state_scr[b] for b in rows]
    o_inter = [[] for _ in rows]
    for n in range(nchunk):
        sl = slice(n * CHUNK, (n + 1) * CHUNK)
        for b in rows:
            o_inter[b].append(dot_nt(q_dec_b[b][sl], state[b].astype(BF16)))
        kv_t = [dot_tn(gv_b[b][sl], k_end[b][sl]) for b in rows]
        state = [state[b] * decay[b][n * CHUNK:n * CHUNK + 1, :] + jnp.where(head_diag, kv_t[b], 0.0)
                 for b in rows]
    o = [o[b] + jnp.concatenate(o_inter[b], axis=0) for b in rows]

    emit_qkv()
    hr = lax.broadcasted_iota(jnp.int32, (GLA_WIDTH, GLA_WIDTH), 0)
    hc = lax.broadcasted_iota(jnp.int32, (GLA_WIDTH, GLA_WIDTH), 1)
    ones_blk = jnp.where((hr // GLA_VDIM) == (hc // GLA_VDIM), 1.0, 0.0).astype(BF16)
    sq_split = [_split_bf16(v * v) for v in o]
    ms = [(dot(hi, ones_blk) + dot(lo, ones_blk)) * (1.0 / GLA_VDIM) for hi, lo in sq_split]
    o = [v * lax.rsqrt(m + EPS) * gng_ref[...] for v, m in zip(o, ms)]
    silu = [g / (1.0 + jnp.exp(-g)) for g in gr]

    emit_qkv(QKV_W // QKV_TILE)
    for b in rows:
        out_ref[b] = jnp.concatenate([y_pool[b], o[b] * silu[b]], axis=1).astype(out_ref.dtype)
        state_scr[b] = state[b]
        halo_scr[b, 0:POOL_HALO, :] = x[b][ts - POOL_HALO:ts, :]


def _inproj_mix(h, g, w, wg2, bg, gng, pw, ps, batch, seq, ts):
    t = batch * seq
    nb = math.gcd(batch, MIX_BATCH_ROWS)
    qkv, y_pg = pl.pallas_call(
        functools.partial(_inproj_mix_kernel, ts=ts, nb=nb),
        grid=(batch // nb, seq // ts),
        in_specs=[
            pl.BlockSpec((nb, ts, D_MODEL), lambda b, s: (b, s, 0)),
            _resident((1, D_MODEL)),
            _resident((D_MODEL, W_IN_COLS)),
            _resident((LANES, GLA_KW)),
            _resident((1, GLA_KW)),
            _resident((1, GLA_WIDTH)),
            _resident((POOL_WIDTH, POOL_WIDTH)),
            _resident((1, POOL_WIDTH)),
        ],
        out_specs=[
            pl.BlockSpec((nb, ts, QKV_W), lambda b, s: (b, s, 0)),
            pl.BlockSpec((nb, ts, POOL_WIDTH + GLA_WIDTH), lambda b, s: (b, s, 0)),
        ],
        out_shape=[
            jax.ShapeDtypeStruct((batch, seq, QKV_W), BF16),
            jax.ShapeDtypeStruct((batch, seq, POOL_WIDTH + GLA_WIDTH), BF16),
        ],
        scratch_shapes=[
            pltpu.VMEM((nb, GLA_WIDTH, GLA_KW), F32),
            pltpu.VMEM((nb, POOL_HALO + ts, POOL_WIDTH), F32),
        ],
        compiler_params=pltpu.CompilerParams(
            dimension_semantics=("parallel", "arbitrary"), vmem_limit_bytes=VMEM_LIMIT),
        name="inproj_mix",
    )(h.reshape(batch, seq, D_MODEL), g, w, wg2, bg, gng, pw, ps)
    return qkv.reshape(t, QKV_W), y_pg.reshape(t, POOL_WIDTH + GLA_WIDTH)


def _out_mlp_kernel(h_ref, ypg_ref, yd_ref, wo_ref, g2_ref, w1_ref, w2_ref, gf_ref, o_ref,
                    *, final):
    npg = POOL_WIDTH + GLA_WIDTH
    h1 = (h_ref[...]
          + jnp.dot(ypg_ref[...], wo_ref[0:npg, :], preferred_element_type=F32)
          + jnp.dot(yd_ref[...], wo_ref[npg:D_MODEL, :], preferred_element_type=F32))
    ms = jnp.mean(h1 * h1, axis=-1, keepdims=True)
    z = (h1 * lax.rsqrt(ms + EPS) * g2_ref[...]).astype(BF16)
    a = jnp.maximum(jnp.dot(z, w1_ref[...], preferred_element_type=F32), 0.0)
    acc = h1 + jnp.dot((a * a).astype(BF16), w2_ref[...], preferred_element_type=F32)
    if final:
        ms = jnp.mean(acc * acc, axis=-1, keepdims=True)
        acc = acc * lax.rsqrt(ms + EPS) * gf_ref[...]
    o_ref[...] = acc


def _out_mlp(h, ypg, yd, wo, g2, w1, w2, gf, tm, final):
    t = h.shape[0]
    return pl.pallas_call(
        functools.partial(_out_mlp_kernel, final=final),
        grid=(t // tm,),
        in_specs=[
            pl.BlockSpec((tm, D_MODEL), lambda i: (i, 0)),
            pl.BlockSpec((tm, POOL_WIDTH + GLA_WIDTH), lambda i: (i, 0)),
            pl.BlockSpec((tm, DIFF_WIDTH), lambda i: (i, 0)),
            _resident((D_MODEL, D_MODEL)),
            _resident((1, D_MODEL)),
            _resident((D_MODEL, D_FF)),
            _resident((D_FF, D_MODEL)),
            _resident((1, D_MODEL)),
        ],
        out_specs=pl.BlockSpec((tm, D_MODEL), lambda i: (i, 0)),
        out_shape=jax.ShapeDtypeStruct((t, D_MODEL), F32),
        compiler_params=pltpu.CompilerParams(
            dimension_semantics=("parallel",), vmem_limit_bytes=VMEM_LIMIT),
        name="out_mlp",
    )(h, ypg, yd, wo, g2, w1, w2, gf)


def _prep_w_in(w):
    pool = w[:, 0:256]
    dq = w[:, 256:768] * (DIFF_QKDIM ** -0.5 * LOG2_E)
    dk = w[:, 768:1280]
    dv = w[:, 1280:1792]
    gq = w[:, 1792:1920]
    gk = w[:, 1920:2048]
    gv = w[:, 2048:2304]
    gr = w[:, 2304:2560]
    gg = w[:, 2560:2576]
    pad = jnp.zeros((D_MODEL, REST_W - REST_GG - GLA_GATE_RANK), w.dtype)
    return jnp.concatenate([dq, dk, dv, pool, gq, gk, gv, gr, gg, pad], axis=1).astype(BF16)


def _prep_w_out(w):
    return jnp.concatenate([w[0:256], w[768:1024], w[256:768]], axis=0).astype(BF16)


def _prep_pool_w(pw):
    out = jnp.zeros((POOL_WIDTH, POOL_WIDTH), F32)
    for g in range(len(POOL_WINDOWS)):
        out = out.at[g * POOL_GDIM:(g + 1) * POOL_GDIM, g * POOL_GDIM:(g + 1) * POOL_GDIM].set(pw[g])
    return out.astype(BF16)


def _tiles(seq, t):
    tm = math.gcd(t, 512)
    tq = math.gcd(seq, 512)
    ts = math.gcd(seq, 256)
    return tm, tq, ts


def kernel(x, norm1_g, w_in, pool_w, pool_scale, diff_lq1, diff_lk1, diff_lq2, diff_lk2, diff_norm_g, gla_w_gate2, gla_b_gate, gla_norm_g, w_out, norm2_g, w_mlp1, w_mlp2, final_norm_g):
    batch, seq, d = x.shape
    assert d == D_MODEL and seq % CHUNK == 0
    depth = w_in.shape[0]
    t = batch * seq
    tm, tq, ts = _tiles(seq, t)
    h = x.reshape(t, D_MODEL)
    gf = final_norm_g.reshape(1, D_MODEL)
    for l in range(depth):
        lambda_init = 0.8 - 0.6 * math.exp(-0.3 * l)
        wg2 = jnp.zeros((LANES, GLA_KW), F32).at[0:GLA_GATE_RANK].set(gla_w_gate2[l])
        qkv, y_pg = _inproj_mix(h, norm1_g[l].reshape(1, D_MODEL), _prep_w_in(w_in[l]), wg2,
                                gla_b_gate[l].reshape(1, GLA_KW),
                                gla_norm_g[l].reshape(1, GLA_WIDTH), _prep_pool_w(pool_w[l]),
                                pool_scale[l].reshape(1, POOL_WIDTH), batch, seq, ts)
        lam = _diff_lambda(diff_lq1[l], diff_lk1[l], diff_lq2[l], diff_lk2[l], lambda_init)
        y_diff = _diff_attention(qkv, lam, diff_norm_g[l].reshape(1, DIFF_WIDTH),
                                 batch, seq, tq, lambda_init)
        h = _out_mlp(h, y_pg, y_diff, _prep_w_out(w_out[l]), norm2_g[l].reshape(1, D_MODEL),
                     w_mlp1[l].astype(BF16), w_mlp2[l].astype(BF16), gf, tm,
                     final=(l == depth - 1))
    return h.reshape(batch, seq, D_MODEL)
```

```python
import functools
import math

import jax
import jax.numpy as jnp
from jax import lax
from jax.experimental import pallas as pl
from jax.experimental.pallas import tpu as pltpu

F32 = jnp.float32
BF16 = jnp.bfloat16

D_MODEL = 1024
CHUNK = 64
POOL_WIDTH = 256
POOL_WINDOWS = (2, 4, 8, 16)
POOL_GDIM = 64
POOL_HALO = 16
DIFF_HEADS = 4
DIFF_VDIM = 128
DIFF_QKDIM = 64
DIFF_WIDTH = 512
GLA_HEADS = 4
GLA_VDIM = 64
GLA_KDIM = 32
GLA_WIDTH = 256
GLA_KW = GLA_HEADS * GLA_KDIM
GLA_GATE_RANK = 16
GLA_GATE_TAU = 16.0
D_FF = 4096
EPS = 1e-6

LANES = 128
QKV_W = 3 * DIFF_WIDTH
REST_POOL = 0
REST_GQ = 256
REST_GK = 384
REST_GV = 512
REST_GR = 768
REST_GG = 1024
REST_W = 1152
W_IN_COLS = QKV_W + REST_W

VMEM_LIMIT = 56 * 1024 * 1024
NEG_BIG = -1e30
LOG2_E = 1.4426950408889634
ATTN_ROW_BLOCK = 256
ATTN_PAIRS_PER_TRIP = 8
MIX_BATCH_ROWS = 4
QKV_TILE = 256


def _resident(shape):
    nd = len(shape)
    return pl.BlockSpec(shape, lambda *_: (0,) * nd, pipeline_mode=pl.Buffered(1))


def _split_bf16(a):
    hi = a.astype(BF16)
    lo = (a - hi.astype(F32)).astype(BF16)
    return hi, lo


def _attn_kernel(q_ref, k_ref, v_ref, lam_ref, g_ref, o_ref,
                 qs_scr, e_scr, kx_scr, sa_scr, sb_scr, la_scr, lb_scr, m_scr, acc_scr,
                 *, tq, rb, nq, lambda_init):
    m_rows = 2 * tq
    nblk = m_rows // rb
    reps = tq // LANES
    nitems = nq * (nq + 1) // 2

    @pl.when(jnp.logical_and(pl.program_id(0) == 0, pl.program_id(1) == 0))
    def _():
        kchunk = lax.broadcasted_iota(jnp.int32, (tq, LANES), 0) // CHUNK
        klane = lax.broadcasted_iota(jnp.int32, (tq, LANES), 1)
        kx_scr[...] = jnp.where(kchunk > klane, 1.0, 0.0).astype(BF16)
        qchunk = (lax.broadcasted_iota(jnp.int32, (m_rows, LANES), 0) % tq) // CHUNK
        qlane = lax.broadcasted_iota(jnp.int32, (m_rows, LANES), 1)
        e_scr[0] = jnp.zeros((m_rows, LANES), BF16)
        e_scr[1] = jnp.where(qchunk == qlane, NEG_BIG, 0.0).astype(BF16)

    acc_scr[...] = jnp.zeros(acc_scr.shape, F32)

    lane = lax.broadcasted_iota(jnp.int32, (tq, LANES), 1)

    def stack_q(i, carry):
        q = q_ref[pl.ds(pl.multiple_of(i * tq, tq), tq), :]
        zero = jnp.zeros_like(q)
        qs_scr[i, 0:tq, :] = jnp.where(lane < DIFF_QKDIM, q, zero)
        qs_scr[i, tq:m_rows, :] = jnp.where(lane >= DIFF_QKDIM, q, zero)
        return carry

    lax.fori_loop(0, nq, stack_q, 0, unroll=math.gcd(nq, 4))

    def qk_block(item, s_dst, r):
        qi, j = item
        rows = slice(r * rb, (r + 1) * rb)
        k = jnp.concatenate([k_ref[pl.ds(pl.multiple_of(j * tq, tq), tq), :], kx_scr[...]],
                            axis=1)
        q = jnp.concatenate([qs_scr[qi, rows, :], e_scr[(j == qi).astype(jnp.int32), rows, :]],
                            axis=1)
        s = lax.dot_general(q, k, (((1,), (1,)), ((), ())), preferred_element_type=F32)
        s_dst[0][rows, :] = s
        lane_max = s[:, 0:LANES]
        for c in range(1, reps):
            lane_max = jnp.maximum(lane_max, s[:, c * LANES:(c + 1) * LANES])
        s_dst[1][rows, :] = lane_max

    def softmax_pv_block(item, s_src, r):
        qi, j = item
        rows = slice(r * rb, (r + 1) * rb)
        m_prev = jnp.where(j == 0, -jnp.inf, m_scr[rows, :])
        m_next = jnp.maximum(m_prev, jnp.max(s_src[1][rows, :], axis=1, keepdims=True))
        alpha = jnp.exp2(m_prev - m_next)
        p = jnp.exp2(s_src[0][rows, :] - jnp.concatenate([m_next] * reps, axis=1))
        m_scr[rows, :] = m_next
        va = jnp.concatenate([v_ref[pl.ds(pl.multiple_of(j * tq, tq), tq), :],
                              jnp.ones((tq, DIFF_VDIM), BF16)], axis=1)
        acc_scr[qi, rows, :] = (jnp.concatenate([alpha, alpha], axis=1) * acc_scr[qi, rows, :]
                                + jnp.dot(p.astype(BF16), va, preferred_element_type=F32))

    def stage(item_next, s_next, item_cur, s_cur):
        for r in range(nblk):
            if item_next is not None:
                qk_block(item_next, s_next, r)
            softmax_pv_block(item_cur, s_cur, r)

    def following(item):
        qi, j = item
        last = j == qi
        qn = jnp.where(last, qi + 1, qi)
        jn = jnp.where(last, 0, j + 1)
        over = qn >= nq
        return jnp.where(over, qi, qn), jnp.where(over, j, jn)

    first = (jnp.int32(0), jnp.int32(0))
    buf_a = (sa_scr, la_scr)
    buf_b = (sb_scr, lb_scr)
    for r in range(nblk):
        qk_block(first, buf_a, r)

    def pair(item0):
        item1 = following(item0)
        item2 = following(item1)
        stage(item1, buf_b, item0, buf_a)
        stage(item2, buf_a, item1, buf_b)
        return item2

    def body(_, item):
        for _ in range(ATTN_PAIRS_PER_TRIP):
            item = pair(item)
        return item

    npairs = nitems // 2
    item = lax.fori_loop(0, npairs // ATTN_PAIRS_PER_TRIP, body, first)
    for _ in range(npairs % ATTN_PAIRS_PER_TRIP):
        item = pair(item)
    if nitems % 2:
        stage(None, None, item, buf_a)

    def finalize(i, carry):
        acc = acc_scr[i]
        a1, l1 = acc[0:tq, 0:DIFF_VDIM], acc[0:tq, DIFF_VDIM:2 * DIFF_VDIM]
        a2, l2 = acc[tq:m_rows, 0:DIFF_VDIM], acc[tq:m_rows, DIFF_VDIM:2 * DIFF_VDIM]
        o = (a1 * l2 - lam_ref[...] * (a2 * l1)) / (l1 * l2)
        ms = jnp.mean(o * o, axis=-1, keepdims=True)
        o = o * lax.rsqrt(ms + EPS) * g_ref[...] * (1.0 - lambda_init)
        o_ref[pl.ds(pl.multiple_of(i * tq, tq), tq), :] = o.astype(o_ref.dtype)
        return carry

    lax.fori_loop(0, nq, finalize, 0, unroll=math.gcd(nq, 4))


def _lambda_kernel(lq1_ref, lk1_ref, lq2_ref, lk2_ref, lam_ref, *, lambda_init):
    a = jnp.sum(lq1_ref[...] * lk1_ref[...], axis=-1, keepdims=True)
    b = jnp.sum(lq2_ref[...] * lk2_ref[...], axis=-1, keepdims=True)
    lam_ref[...] = jnp.broadcast_to(jnp.exp(a) - jnp.exp(b) + lambda_init, lam_ref.shape)


def _diff_lambda(lq1, lk1, lq2, lk2, lambda_init):
    args = [a.reshape(1, DIFF_QKDIM) for a in (lq1, lk1, lq2, lk2)]
    return pl.pallas_call(
        functools.partial(_lambda_kernel, lambda_init=lambda_init),
        out_shape=jax.ShapeDtypeStruct((1, LANES), F32),
        name="diff_lambda",
    )(*args)


def _diff_attention(qkv, lam, norm_g, batch, seq, tq, lambda_init):
    t = batch * seq
    nq = seq // tq
    kcol = DIFF_WIDTH // DIFF_VDIM
    vcol = 2 * DIFF_WIDTH // DIFF_VDIM
    return pl.pallas_call(
        functools.partial(_attn_kernel, tq=tq, rb=min(ATTN_ROW_BLOCK, 2 * tq), nq=nq,
                          lambda_init=lambda_init),
        grid=(batch, DIFF_HEADS),
        in_specs=[
            pl.BlockSpec((seq, DIFF_VDIM), lambda b, h: (b, h)),
            pl.BlockSpec((seq, DIFF_VDIM), lambda b, h: (b, kcol + h)),
            pl.BlockSpec((seq, DIFF_VDIM), lambda b, h: (b, vcol + h)),
            pl.BlockSpec((1, LANES), lambda b, h: (0, 0)),
            pl.BlockSpec((1, DIFF_VDIM), lambda b, h: (0, h)),
        ],
        out_specs=pl.BlockSpec((seq, DIFF_VDIM), lambda b, h: (b, h)),
        out_shape=jax.ShapeDtypeStruct((t, DIFF_WIDTH), BF16),
        scratch_shapes=[
            pltpu.VMEM((nq, 2 * tq, LANES), BF16),
            pltpu.VMEM((2, 2 * tq, LANES), BF16),
            pltpu.VMEM((tq, LANES), BF16),
            pltpu.VMEM((2 * tq, tq), F32),
            pltpu.VMEM((2 * tq, tq), F32),
            pltpu.VMEM((2 * tq, LANES), F32),
            pltpu.VMEM((2 * tq, LANES), F32),
            pltpu.VMEM((2 * tq, LANES), F32),
            pltpu.VMEM((nq, 2 * tq, 2 * DIFF_VDIM), F32),
        ],
        compiler_params=pltpu.CompilerParams(
            dimension_semantics=("arbitrary", "arbitrary"), vmem_limit_bytes=VMEM_LIMIT),
        name="diff_attention",
    )(qkv, qkv, qkv, lam, norm_g)


def _inproj_mix_kernel(h_ref, g_ref, w_ref, wg2_ref, bg_ref, gng_ref, pw_ref, ps_ref,
                       qkv_ref, out_ref, state_scr, halo_scr, *, ts, nb):
    st = pl.program_id(1)
    nchunk = ts // CHUNK
    rows = range(nb)

    @pl.when(st == 0)
    def _():
        state_scr[...] = jnp.zeros_like(state_scr)
        halo_scr[:, 0:POOL_HALO, :] = jnp.zeros((nb, POOL_HALO, POOL_WIDTH), F32)

    def dot(a, b):
        return jnp.dot(a, b, preferred_element_type=F32)

    def dot_nt(a, b):
        return lax.dot_general(a, b, (((1,), (1,)), ((), ())), preferred_element_type=F32)

    def dot_tn(a, b):
        return lax.dot_general(a, b, (((0,), (0,)), ((), ())), preferred_element_type=F32)

    h = h_ref[...].reshape(nb * ts, D_MODEL)
    ms = jnp.mean(h * h, axis=-1, keepdims=True)
    u = (h * lax.rsqrt(ms + EPS) * g_ref[...]).astype(BF16)
    rest_all = dot(u, w_ref[:, QKV_W:])
    rest = [rest_all[b * ts:(b + 1) * ts] for b in rows]

    qkv_tiles = iter(range(QKV_W // QKV_TILE))

    def emit_qkv(count=1):
        for _ in range(count):
            c = next(qkv_tiles, None)
            if c is not None:
                cols = slice(c * QKV_TILE, (c + 1) * QKV_TILE)
                y = dot(u, w_ref[:, cols]).astype(BF16)
                for b in rows:
                    qkv_ref[b, :, cols] = y[b * ts:(b + 1) * ts]

    x = [rest[b][:, REST_POOL:REST_POOL + POOL_WIDTH] for b in rows]
    for b in rows:
        halo_scr[b, POOL_HALO:POOL_HALO + ts, :] = x[b]

    def window_sums(b):
        xf = halo_scr[b]
        xa = xf[:, 0:LANES]
        c2 = xa + pltpu.roll(xa, 1, axis=0)
        c4 = c2 + pltpu.roll(c2, 2, axis=0)
        xb = xf[:, LANES:2 * LANES]
        d2 = xb + pltpu.roll(xb, 1, axis=0)
        d4 = d2 + pltpu.roll(d2, 2, axis=0)
        c8 = d4 + pltpu.roll(d4, 4, axis=0)
        c16 = c8 + pltpu.roll(c8, 8, axis=0)
        cut = slice(POOL_HALO, POOL_HALO + ts)
        return xa[cut], xb[cut], c2[cut], c4[cut], c8[cut], c16[cut]

    sums = [window_sums(b) for b in rows]
    tpos = (st * ts + lax.broadcasted_iota(jnp.int32, (ts, LANES), 0) + 1).astype(F32)
    lane = lax.broadcasted_iota(jnp.int32, (ts, LANES), 1)
    first = lane < POOL_GDIM
    inv_a = jnp.where(first, 1.0 / jnp.minimum(tpos, 2.0), 1.0 / jnp.minimum(tpos, 4.0))
    inv_b = jnp.where(first, 1.0 / jnp.minimum(tpos, 8.0), 1.0 / jnp.minimum(tpos, 16.0))

    def pooled_of(xa, xb, c2, c4, c8, c16):
        mean_a = jnp.where(first, c2, c4) * inv_a
        mean_b = jnp.where(first, c8, c16) * inv_b
        return jnp.concatenate([mean_a - xa, mean_b - xb], axis=1).astype(BF16)

    pooled = [pooled_of(*s) for s in sums]
    pw = pw_ref[...]
    y_pool = [dot(p, pw) * ps_ref[...] for p in pooled]

    emit_qkv()
    gq = [rest[b][:, REST_GQ:REST_GQ + GLA_KW] for b in rows]
    gk = [rest[b][:, REST_GK:REST_GK + GLA_KW] for b in rows]
    gv_b = [rest[b][:, REST_GV:REST_GV + GLA_WIDTH].astype(BF16) for b in rows]
    gr = [rest[b][:, REST_GR:REST_GR + GLA_WIDTH] for b in rows]
    gg = [rest[b][:, REST_GG:REST_GG + LANES] for b in rows]

    w_hi, w_lo = _split_bf16(wg2_ref[...])
    w_cat = jnp.concatenate([w_hi, w_lo], axis=1)
    g_split = [_split_bf16(v) for v in gg]
    xg = []
    for hi, lo in g_split:
        hh = dot(hi, w_cat)
        xg.append(hh[:, 0:GLA_KW] + hh[:, GLA_KW:2 * GLA_KW] + dot(lo, w_hi) + bg_ref[...])
    log_a = [(jnp.minimum(v, 0.0) - jnp.log(1.0 + jnp.exp(-jnp.abs(v)))) * (1.0 / GLA_GATE_TAU)
             for v in xg]

    emit_qkv()
    r = lax.broadcasted_iota(jnp.int32, (ts, ts), 0)
    c = lax.broadcasted_iota(jnp.int32, (ts, ts), 1)
    same_chunk = (r // CHUNK) == (c // CHUNK)
    causal = jnp.logical_and(same_chunk, c <= r)
    tri = jnp.where(causal, 1.0, 0.0).astype(BF16)
    blk = jnp.where(same_chunk, 1.0, 0.0).astype(BF16)
    la_cat = [jnp.concatenate(_split_bf16(v), axis=1) for v in log_a]
    bcum2 = [dot(tri, v) for v in la_cat]
    btot2 = [dot(blk, v) for v in la_cat]
    bcum = [v[:, 0:GLA_KW] + v[:, GLA_KW:2 * GLA_KW] for v in bcum2]
    btot = [v[:, 0:GLA_KW] + v[:, GLA_KW:2 * GLA_KW] for v in btot2]

    q_dec_b = [((q * (GLA_KDIM ** -0.5)) * jnp.exp(bc)).astype(BF16) for q, bc in zip(gq, bcum)]
    k_inv = [(k * jnp.exp(-bc)).astype(BF16) for k, bc in zip(gk, bcum)]
    k_end = [(k * jnp.exp(bt - bc)).astype(BF16) for k, bt, bc in zip(gk, btot, bcum)]
    decay = [jnp.exp(bt) for bt in btot]

    emit_qkv()
    klane = lax.broadcasted_iota(jnp.int32, (ts, GLA_KW), 1)
    vlane = lax.broadcasted_iota(jnp.int32, (ts, GLA_WIDTH), 1)
    o = [None] * nb
    for h in range(GLA_HEADS):
        qh = [jnp.where(klane // GLA_KDIM == h, q, jnp.zeros_like(q)) for q in q_dec_b]
        att = [jnp.where(causal, dot_nt(q, k), 0.0).astype(BF16) for q, k in zip(qh, k_inv)]
        vh = [jnp.where(vlane // GLA_VDIM == h, v, jnp.zeros_like(v)) for v in gv_b]
        part = [dot(a, v) for a, v in zip(att, vh)]
        o = part if h == 0 else [a + b for a, b in zip(o, part)]
        if h % 2 == 1:
            emit_qkv()

    srow = lax.broadcasted_iota(jnp.int32, (GLA_WIDTH, GLA_KW), 0)
    scol = lax.broadcasted_iota(jnp.int32, (GLA_WIDTH, GLA_KW), 1)
    head_diag = (srow // GLA_VDIM) == (scol // GLA_KDIM)
    state = [state_scr[b] for b in rows]
    o_inter = [[] for _ in rows]
    for n in range(nchunk):
        sl = slice(n * CHUNK, (n + 1) * CHUNK)
        for b in rows:
            o_inter[b].append(dot_nt(q_dec_b[b][sl], state[b].astype(BF16)))
        kv_t = [dot_tn(gv_b[b][sl], k_end[b][sl]) for b in rows]
        state = [state[b] * decay[b][n * CHUNK:n * CHUNK + 1, :] + jnp.where(head_diag, kv_t[b], 0.0)
                 for b in rows]
    o = [o[b] + jnp.concatenate(o_inter[b], axis=0) for b in rows]

    emit_qkv()
    hr = lax.broadcasted_iota(jnp.int32, (GLA_WIDTH, GLA_WIDTH), 0)
    hc = lax.broadcasted_iota(jnp.int32, (GLA_WIDTH, GLA_WIDTH), 1)
    ones_blk = jnp.where((hr // GLA_VDIM) == (hc // GLA_VDIM), 1.0, 0.0).astype(BF16)
    sq_split = [_split_bf16(v * v) for v in o]
    ms = [(dot(hi, ones_blk) + dot(lo, ones_blk)) * (1.0 / GLA_VDIM) for hi, lo in sq_split]
    o = [v * lax.rsqrt(m + EPS) * gng_ref[...] for v, m in zip(o, ms)]
    silu = [g / (1.0 + jnp.exp(-g)) for g in gr]

    emit_qkv(QKV_W // QKV_TILE)
    for b in rows:
        out_ref[b] = jnp.concatenate([y_pool[b], o[b] * silu[b]], axis=1).astype(out_ref.dtype)
        state_scr[b] = state[b]
        halo_scr[b, 0:POOL_HALO, :] = x[b][ts - POOL_HALO:ts, :]


def _inproj_mix(h, g, w, wg2, bg, gng, pw, ps, batch, seq, ts):
    t = batch * seq
    nb = math.gcd(batch, MIX_BATCH_ROWS)
    qkv, y_pg = pl.pallas_call(
        functools.partial(_inproj_mix_kernel, ts=ts, nb=nb),
        grid=(batch // nb, seq // ts),
        in_specs=[
            pl.BlockSpec((nb, ts, D_MODEL), lambda b, s: (b, s, 0)),
            _resident((1, D_MODEL)),
            _resident((D_MODEL, W_IN_COLS)),
            _resident((LANES, GLA_KW)),
            _resident((1, GLA_KW)),
            _resident((1, GLA_WIDTH)),
            _resident((POOL_WIDTH, POOL_WIDTH)),
            _resident((1, POOL_WIDTH)),
        ],
        out_specs=[
            pl.BlockSpec((nb, ts, QKV_W), lambda b, s: (b, s, 0)),
            pl.BlockSpec((nb, ts, POOL_WIDTH + GLA_WIDTH), lambda b, s: (b, s, 0)),
        ],
        out_shape=[
            jax.ShapeDtypeStruct((batch, seq, QKV_W), BF16),
            jax.ShapeDtypeStruct((batch, seq, POOL_WIDTH + GLA_WIDTH), BF16),
        ],
        scratch_shapes=[
            pltpu.VMEM((nb, GLA_WIDTH, GLA_KW), F32),
            pltpu.VMEM((nb, POOL_HALO + ts, POOL_WIDTH), F32),
        ],
        compiler_params=pltpu.CompilerParams(
            dimension_semantics=("parallel", "arbitrary"), vmem_limit_bytes=VMEM_LIMIT),
        name="inproj_mix",
    )(h.reshape(batch, seq, D_MODEL), g, w, wg2, bg, gng, pw, ps)
    return qkv.reshape(t, QKV_W), y_pg.reshape(t, POOL_WIDTH + GLA_WIDTH)


def _out_mlp_kernel(h_ref, ypg_ref, yd_ref, wo_ref, g2_ref, w1_ref, w2_ref, gf_ref, o_ref,
                    *, final):
    npg = POOL_WIDTH + GLA_WIDTH
    h1 = (h_ref[...]
          + jnp.dot(ypg_ref[...], wo_ref[0:npg, :], preferred_element_type=F32)
          + jnp.dot(yd_ref[...], wo_ref[npg:D_MODEL, :], preferred_element_type=F32))
    ms = jnp.mean(h1 * h1, axis=-1, keepdims=True)
    z = (h1 * lax.rsqrt(ms + EPS) * g2_ref[...]).astype(BF16)
    a = jnp.maximum(jnp.dot(z, w1_ref[...], preferred_element_type=F32), 0.0)
    acc = h1 + jnp.dot((a * a).astype(BF16), w2_ref[...], preferred_element_type=F32)
    if final:
        ms = jnp.mean(acc * acc, axis=-1, keepdims=True)
        acc = acc * lax.rsqrt(ms + EPS) * gf_ref[...]
    o_ref[...] = acc


def _out_mlp(h, ypg, yd, wo, g2, w1, w2, gf, tm, final):
    t = h.shape[0]
    return pl.pallas_call(
        functools.partial(_out_mlp_kernel, final=final),
        grid=(t // tm,),
        in_specs=[
            pl.BlockSpec((tm, D_MODEL), lambda i: (i, 0)),
            pl.BlockSpec((tm, POOL_WIDTH + GLA_WIDTH), lambda i: (i, 0)),
            pl.BlockSpec((tm, DIFF_WIDTH), lambda i: (i, 0)),
            _resident((D_MODEL, D_MODEL)),
            _resident((1, D_MODEL)),
            _resident((D_MODEL, D_FF)),
            _resident((D_FF, D_MODEL)),
            _resident((1, D_MODEL)),
        ],
        out_specs=pl.BlockSpec((tm, D_MODEL), lambda i: (i, 0)),
        out_shape=jax.ShapeDtypeStruct((t, D_MODEL), F32),
        compiler_params=pltpu.CompilerParams(
            dimension_semantics=("parallel",), vmem_limit_bytes=VMEM_LIMIT),
        name="out_mlp",
    )(h, ypg, yd, wo, g2, w1, w2, gf)


def _prep_w_in(w):
    pool = w[:, 0:256]
    dq = w[:, 256:768] * (DIFF_QKDIM ** -0.5 * LOG2_E)
    dk = w[:, 768:1280]
    dv = w[:, 1280:1792]
    gq = w[:, 1792:1920]
    gk = w[:, 1920:2048]
    gv = w[:, 2048:2304]
    gr = w[:, 2304:2560]
    gg = w[:, 2560:2576]
    pad = jnp.zeros((D_MODEL, REST_W - REST_GG - GLA_GATE_RANK), w.dtype)
    return jnp.concatenate([dq, dk, dv, pool, gq, gk, gv, gr, gg, pad], axis=1).astype(BF16)


def _prep_w_out(w):
    return jnp.concatenate([w[0:256], w[768:1024], w[256:768]], axis=0).astype(BF16)


def _prep_pool_w(pw):
    out = jnp.zeros((POOL_WIDTH, POOL_WIDTH), F32)
    for g in range(len(POOL_WINDOWS)):
        out = out.at[g * POOL_GDIM:(g + 1) * POOL_GDIM, g * POOL_GDIM:(g + 1) * POOL_GDIM].set(pw[g])
    return out.astype(BF16)


def _tiles(seq, t):
    tm = math.gcd(t, 512)
    tq = math.gcd(seq, 512)
    ts = math.gcd(seq, 256)
    return tm, tq, ts


def kernel(x, norm1_g, w_in, pool_w, pool_scale, diff_lq1, diff_lk1, diff_lq2, diff_lk2, diff_norm_g, gla_w_gate2, gla_b_gate, gla_norm_g, w_out, norm2_g, w_mlp1, w_mlp2, final_norm_g):
    batch, seq, d = x.shape
    assert d == D_MODEL and seq % CHUNK == 0
    depth = w_in.shape[0]
    t = batch * seq
    tm, tq, ts = _tiles(seq, t)
    h = x.reshape(t, D_MODEL)
    gf = final_norm_g.reshape(1, D_MODEL)
    for l in range(depth):
        lambda_init = 0.8 - 0.6 * math.exp(-0.3 * l)
        wg2 = jnp.zeros((LANES, GLA_KW), F32).at[0:GLA_GATE_RANK].set(gla_w_gate2[l])
        qkv, y_pg = _inproj_mix(h, norm1_g[l].reshape(1, D_MODEL), _prep_w_in(w_in[l]), wg2,
                                gla_b_gate[l].reshape(1, GLA_KW),
                                gla_norm_g[l].reshape(1, GLA_WIDTH), _prep_pool_w(pool_w[l]),
                                pool_scale[l].reshape(1, POOL_WIDTH), batch, seq, ts)
        lam = _diff_lambda(diff_lq1[l], diff_lk1[l], diff_lq2[l], diff_lk2[l], lambda_init)
        y_diff = _diff_attention(qkv, lam, diff_norm_g[l].reshape(1, DIFF_WIDTH),
                                 batch, seq, tq, lambda_init)
        h = _out_mlp(h, y_pg, y_diff, _prep_w_out(w_out[l]), norm2_g[l].reshape(1, D_MODEL),
                     w_mlp1[l].astype(BF16), w_mlp2[l].astype(BF16), gf, tm,
                     final=(l == depth - 1))
    return h.reshape(batch, seq, D_MODEL)
```

```python
import functools
import math

import jax
import jax.numpy as jnp
from jax import lax
from jax.experimental import pallas as pl
from jax.experimental.pallas import tpu as pltpu

F32 = jnp.float32
BF16 = jnp.bfloat16

D_MODEL = 1024
CHUNK = 64
POOL_WIDTH = 256
POOL_WINDOWS = (2, 4, 8, 16)
POOL_GDIM = 64
POOL_HALO = 16
DIFF_HEADS = 4
DIFF_VDIM = 128
DIFF_QKDIM = 64
DIFF_WIDTH = 512
GLA_HEADS = 4
GLA_VDIM = 64
GLA_KDIM = 32
GLA_WIDTH = 256
GLA_KW = GLA_HEADS * GLA_KDIM
GLA_GATE_RANK = 16
GLA_GATE_TAU = 16.0
D_FF = 4096
EPS = 1e-6

LANES = 128
QKV_W = 3 * DIFF_WIDTH
REST_POOL = 0
REST_GQ = 256
REST_GK = 384
REST_GV = 512
REST_GR = 768
REST_GG = 1024
REST_W = 1152
W_IN_COLS = QKV_W + REST_W

VMEM_LIMIT = 56 * 1024 * 1024
NEG_BIG = -1e30
LOG2_E = 1.4426950408889634
ATTN_ROW_BLOCK = 256
ATTN_PAIRS_PER_TRIP = 8
MIX_BATCH_ROWS = 4
QKV_TILE = 256


def _resident(shape):
    nd = len(shape)
    return pl.BlockSpec(shape, lambda *_: (0,) * nd, pipeline_mode=pl.Buffered(1))


def _split_bf16(a):
    hi = a.astype(BF16)
    lo = (a - hi.astype(F32)).astype(BF16)
    return hi, lo


def _attn_kernel(q_ref, k_ref, v_ref, lam_ref, g_ref, o_ref,
                 e_scr, kx_scr, sa_scr, sb_scr, la_scr, lb_scr, m_scr, acc_scr,
                 *, tq, rb, nq, lambda_init):
    m_rows = 2 * tq
    nblk = m_rows // rb
    reps = tq // LANES
    nitems = nq * (nq + 1) // 2

    @pl.when(jnp.logical_and(pl.program_id(0) == 0, pl.program_id(1) == 0))
    def _():
        kchunk = lax.broadcasted_iota(jnp.int32, (tq, LANES), 0) // CHUNK
        klane = lax.broadcasted_iota(jnp.int32, (tq, LANES), 1)
        kx_scr[...] = jnp.where(kchunk > klane, 1.0, 0.0).astype(BF16)
        qchunk = (lax.broadcasted_iota(jnp.int32, (m_rows, LANES), 0) % tq) // CHUNK
        qlane = lax.broadcasted_iota(jnp.int32, (m_rows, LANES), 1)
        e_scr[0] = jnp.zeros((m_rows, LANES), BF16)
        e_scr[1] = jnp.where(qchunk == qlane, NEG_BIG, 0.0).astype(BF16)

    acc_scr[...] = jnp.zeros(acc_scr.shape, F32)

    lane = lax.broadcasted_iota(jnp.int32, (rb, LANES), 1)

    def qk_block(item, s_dst, r):
        qi, j = item
        rows = slice(r * rb, (r + 1) * rb)
        k = jnp.concatenate([k_ref[pl.ds(pl.multiple_of(j * tq, tq), tq), :], kx_scr[...]],
                            axis=1)
        second_map, row0 = divmod(r * rb, tq)
        q_raw = q_ref[pl.ds(pl.multiple_of(qi * tq + row0, rb), rb), :]
        keep = (lane >= DIFF_QKDIM) if second_map else (lane < DIFF_QKDIM)
        q_half = jnp.where(keep, q_raw, jnp.zeros_like(q_raw))
        q = jnp.concatenate([q_half, e_scr[(j == qi).astype(jnp.int32), rows, :]], axis=1)
        s = lax.dot_general(q, k, (((1,), (1,)), ((), ())), preferred_element_type=F32)
        s_dst[0][rows, :] = s
        lane_max = s[:, 0:LANES]
        for c in range(1, reps):
            lane_max = jnp.maximum(lane_max, s[:, c * LANES:(c + 1) * LANES])
        s_dst[1][rows, :] = lane_max

    def softmax_pv_block(item, s_src, r):
        qi, j = item
        rows = slice(r * rb, (r + 1) * rb)
        m_prev = jnp.where(j == 0, -jnp.inf, m_scr[rows, :])
        m_next = jnp.maximum(m_prev, jnp.max(s_src[1][rows, :], axis=1, keepdims=True))
        alpha = jnp.exp2(m_prev - m_next)
        p = jnp.exp2(s_src[0][rows, :] - jnp.concatenate([m_next] * reps, axis=1))
        m_scr[rows, :] = m_next
        va = jnp.concatenate([v_ref[pl.ds(pl.multiple_of(j * tq, tq), tq), :],
                              jnp.ones((tq, DIFF_VDIM), BF16)], axis=1)
        acc_scr[qi, rows, :] = (jnp.concatenate([alpha, alpha], axis=1) * acc_scr[qi, rows, :]
                                + jnp.dot(p.astype(BF16), va, preferred_element_type=F32))

    def stage(item_next, s_next, item_cur, s_cur):
        for r in range(nblk):
            if item_next is not None:
                qk_block(item_next, s_next, r)
            softmax_pv_block(item_cur, s_cur, r)

    def following(item):
        qi, j = item
        last = j == qi
        qn = jnp.where(last, qi + 1, qi)
        jn = jnp.where(last, 0, j + 1)
        over = qn >= nq
        return jnp.where(over, qi, qn), jnp.where(over, j, jn)

    first = (jnp.int32(0), jnp.int32(0))
    buf_a = (sa_scr, la_scr)
    buf_b = (sb_scr, lb_scr)
    for r in range(nblk):
        qk_block(first, buf_a, r)

    def pair(item0):
        item1 = following(item0)
        item2 = following(item1)
        stage(item1, buf_b, item0, buf_a)
        stage(item2, buf_a, item1, buf_b)
        return item2

    def body(_, item):
        for _ in range(ATTN_PAIRS_PER_TRIP):
            item = pair(item)
        return item

    npairs = nitems // 2
    item = lax.fori_loop(0, npairs // ATTN_PAIRS_PER_TRIP, body, first)
    for _ in range(npairs % ATTN_PAIRS_PER_TRIP):
        item = pair(item)
    if nitems % 2:
        stage(None, None, item, buf_a)

    def finalize(i, carry):
        acc = acc_scr[i]
        a1, l1 = acc[0:tq, 0:DIFF_VDIM], acc[0:tq, DIFF_VDIM:2 * DIFF_VDIM]
        a2, l2 = acc[tq:m_rows, 0:DIFF_VDIM], acc[tq:m_rows, DIFF_VDIM:2 * DIFF_VDIM]
        o = (a1 * l2 - lam_ref[...] * (a2 * l1)) / (l1 * l2)
        ms = jnp.mean(o * o, axis=-1, keepdims=True)
        o = o * lax.rsqrt(ms + EPS) * g_ref[...] * (1.0 - lambda_init)
        o_ref[pl.ds(pl.multiple_of(i * tq, tq), tq), :] = o.astype(o_ref.dtype)
        return carry

    lax.fori_loop(0, nq, finalize, 0, unroll=math.gcd(nq, 4))


def _lambda_kernel(lq1_ref, lk1_ref, lq2_ref, lk2_ref, lam_ref, *, lambda_init):
    a = jnp.sum(lq1_ref[...] * lk1_ref[...], axis=-1, keepdims=True)
    b = jnp.sum(lq2_ref[...] * lk2_ref[...], axis=-1, keepdims=True)
    lam_ref[...] = jnp.broadcast_to(jnp.exp(a) - jnp.exp(b) + lambda_init, lam_ref.shape)


def _diff_lambda(lq1, lk1, lq2, lk2, lambda_init):
    args = [a.reshape(1, DIFF_QKDIM) for a in (lq1, lk1, lq2, lk2)]
    return pl.pallas_call(
        functools.partial(_lambda_kernel, lambda_init=lambda_init),
        out_shape=jax.ShapeDtypeStruct((1, LANES), F32),
        name="diff_lambda",
    )(*args)


def _diff_attention(qkv, lam, norm_g, batch, seq, tq, lambda_init):
    t = batch * seq
    nq = seq // tq
    kcol = DIFF_WIDTH // DIFF_VDIM
    vcol = 2 * DIFF_WIDTH // DIFF_VDIM
    return pl.pallas_call(
        functools.partial(_attn_kernel, tq=tq, rb=math.gcd(ATTN_ROW_BLOCK, tq), nq=nq,
                          lambda_init=lambda_init),
        grid=(batch, DIFF_HEADS),
        in_specs=[
            pl.BlockSpec((seq, DIFF_VDIM), lambda b, h: (b, h)),
            pl.BlockSpec((seq, DIFF_VDIM), lambda b, h: (b, kcol + h)),
            pl.BlockSpec((seq, DIFF_VDIM), lambda b, h: (b, vcol + h)),
            pl.BlockSpec((1, LANES), lambda b, h: (0, 0)),
            pl.BlockSpec((1, DIFF_VDIM), lambda b, h: (0, h)),
        ],
        out_specs=pl.BlockSpec((seq, DIFF_VDIM), lambda b, h: (b, h)),
        out_shape=jax.ShapeDtypeStruct((t, DIFF_WIDTH), BF16),
        scratch_shapes=[
            pltpu.VMEM((2, 2 * tq, LANES), BF16),
            pltpu.VMEM((tq, LANES), BF16),
            pltpu.VMEM((2 * tq, tq), F32),
            pltpu.VMEM((2 * tq, tq), F32),
            pltpu.VMEM((2 * tq, LANES), F32),
            pltpu.VMEM((2 * tq, LANES), F32),
            pltpu.VMEM((2 * tq, LANES), F32),
            pltpu.VMEM((nq, 2 * tq, 2 * DIFF_VDIM), F32),
        ],
        compiler_params=pltpu.CompilerParams(
            dimension_semantics=("arbitrary", "arbitrary"), vmem_limit_bytes=VMEM_LIMIT),
        name="diff_attention",
    )(qkv, qkv, qkv, lam, norm_g)


def _inproj_mix_kernel(h_ref, g_ref, w_ref, wg2_ref, bg_ref, gng_ref, pw_ref, ps_ref,
                       qkv_ref, out_ref, state_scr, halo_scr, *, ts, nb):
    st = pl.program_id(1)
    nchunk = ts // CHUNK
    rows = range(nb)

    @pl.when(st == 0)
    def _():
        state_scr[...] = jnp.zeros_like(state_scr)
        halo_scr[:, 0:POOL_HALO, :] = jnp.zeros((nb, POOL_HALO, POOL_WIDTH), F32)

    def dot(a, b):
        return jnp.dot(a, b, preferred_element_type=F32)

    def dot_nt(a, b):
        return lax.dot_general(a, b, (((1,), (1,)), ((), ())), preferred_element_type=F32)

    def dot_tn(a, b):
        return lax.dot_general(a, b, (((0,), (0,)), ((), ())), preferred_element_type=F32)

    h = h_ref[...].reshape(nb * ts, D_MODEL)
    ms = jnp.mean(h * h, axis=-1, keepdims=True)
    u = (h * lax.rsqrt(ms + EPS) * g_ref[...]).astype(BF16)
    rest_all = dot(u, w_ref[:, QKV_W:])
    rest = [rest_all[b * ts:(b + 1) * ts] for b in rows]

    qkv_tiles = iter(range(QKV_W // QKV_TILE))

    def emit_qkv(count=1):
        for _ in range(count):
            c = next(qkv_tiles, None)
            if c is not None:
                cols = slice(c * QKV_TILE, (c + 1) * QKV_TILE)
                y = dot(u, w_ref[:, cols]).astype(BF16)
                for b in rows:
                    qkv_ref[b, :, cols] = y[b * ts:(b + 1) * ts]

    x = [rest[b][:, REST_POOL:REST_POOL + POOL_WIDTH] for b in rows]
    for b in rows:
        halo_scr[b, POOL_HALO:POOL_HALO + ts, :] = x[b]

    def window_sums(b):
        xf = halo_scr[b]
        xa = xf[:, 0:LANES]
        c2 = xa + pltpu.roll(xa, 1, axis=0)
        c4 = c2 + pltpu.roll(c2, 2, axis=0)
        xb = xf[:, LANES:2 * LANES]
        d2 = xb + pltpu.roll(xb, 1, axis=0)
        d4 = d2 + pltpu.roll(d2, 2, axis=0)
        c8 = d4 + pltpu.roll(d4, 4, axis=0)
        c16 = c8 + pltpu.roll(c8, 8, axis=0)
        cut = slice(POOL_HALO, POOL_HALO + ts)
        return xa[cut], xb[cut], c2[cut], c4[cut], c8[cut], c16[cut]

    sums = [window_sums(b) for b in rows]
    tpos = (st * ts + lax.broadcasted_iota(jnp.int32, (ts, LANES), 0) + 1).astype(F32)
    lane = lax.broadcasted_iota(jnp.int32, (ts, LANES), 1)
    first = lane < POOL_GDIM
    inv_a = jnp.where(first, 1.0 / jnp.minimum(tpos, 2.0), 1.0 / jnp.minimum(tpos, 4.0))
    inv_b = jnp.where(first, 1.0 / jnp.minimum(tpos, 8.0), 1.0 / jnp.minimum(tpos, 16.0))

    def pooled_of(xa, xb, c2, c4, c8, c16):
        mean_a = jnp.where(first, c2, c4) * inv_a
        mean_b = jnp.where(first, c8, c16) * inv_b
        return jnp.concatenate([mean_a - xa, mean_b - xb], axis=1).astype(BF16)

    pooled = [pooled_of(*s) for s in sums]
    pw = pw_ref[...]
    y_pool = [dot(p, pw) * ps_ref[...] for p in pooled]

    emit_qkv()
    gq = [rest[b][:, REST_GQ:REST_GQ + GLA_KW] for b in rows]
    gk = [rest[b][:, REST_GK:REST_GK + GLA_KW] for b in rows]
    gv_b = [rest[b][:, REST_GV:REST_GV + GLA_WIDTH].astype(BF16) for b in rows]
    gr = [rest[b][:, REST_GR:REST_GR + GLA_WIDTH] for b in rows]
    gg = [rest[b][:, REST_GG:REST_GG + LANES] for b in rows]

    w_hi, w_lo = _split_bf16(wg2_ref[...])
    w_cat = jnp.concatenate([w_hi, w_lo], axis=1)
    g_split = [_split_bf16(v) for v in gg]
    xg = []
    for hi, lo in g_split:
        hh = dot(hi, w_cat)
        xg.append(hh[:, 0:GLA_KW] + hh[:, GLA_KW:2 * GLA_KW] + dot(lo, w_hi) + bg_ref[...])
    log_a = [(jnp.minimum(v, 0.0) - jnp.log(1.0 + jnp.exp(-jnp.abs(v)))) * (1.0 / GLA_GATE_TAU)
             for v in xg]

    emit_qkv()
    r = lax.broadcasted_iota(jnp.int32, (ts, ts), 0)
    c = lax.broadcasted_iota(jnp.int32, (ts, ts), 1)
    same_chunk = (r // CHUNK) == (c // CHUNK)
    causal = jnp.logical_and(same_chunk, c <= r)
    tri = jnp.where(causal, 1.0, 0.0).astype(BF16)
    blk = jnp.where(same_chunk, 1.0, 0.0).astype(BF16)
    la_cat = [jnp.concatenate(_split_bf16(v), axis=1) for v in log_a]
    bcum2 = [dot(tri, v) for v in la_cat]
    btot2 = [dot(blk, v) for v in la_cat]
    bcum = [v[:, 0:GLA_KW] + v[:, GLA_KW:2 * GLA_KW] for v in bcum2]
    btot = [v[:, 0:GLA_KW] + v[:, GLA_KW:2 * GLA_KW] for v in btot2]

    q_dec_b = [((q * (GLA_KDIM ** -0.5)) * jnp.exp(bc)).astype(BF16) for q, bc in zip(gq, bcum)]
    k_inv = [(k * jnp.exp(-bc)).astype(BF16) for k, bc in zip(gk, bcum)]
    k_end = [(k * jnp.exp(bt - bc)).astype(BF16) for k, bt, bc in zip(gk, btot, bcum)]
    decay = [jnp.exp(bt) for bt in btot]

    emit_qkv()
    klane = lax.broadcasted_iota(jnp.int32, (ts, GLA_KW), 1)
    vlane = lax.broadcasted_iota(jnp.int32, (ts, GLA_WIDTH), 1)
    o = [None] * nb
    for h in range(GLA_HEADS):
        qh = [jnp.where(klane // GLA_KDIM == h, q, jnp.zeros_like(q)) for q in q_dec_b]
        att = [jnp.where(causal, dot_nt(q, k), 0.0).astype(BF16) for q, k in zip(qh, k_inv)]
        vh = [jnp.where(vlane // GLA_VDIM == h, v, jnp.zeros_like(v)) for v in gv_b]
        part = [dot(a, v) for a, v in zip(att, vh)]
        o = part if h == 0 else [a + b for a, b in zip(o, part)]
        if h % 2 == 1:
            emit_qkv()

    srow = lax.broadcasted_iota(jnp.int32, (GLA_WIDTH, GLA_KW), 0)
    scol = lax.broadcasted_iota(jnp.int32, (GLA_WIDTH, GLA_KW), 1)
    head_diag = (srow // GLA_VDIM) == (scol // GLA_KDIM)
    state = [state_scr[b] for b in rows]
    o_inter = [[] for _ in rows]
    for n in range(nchunk):
        sl = slice(n * CHUNK, (n + 1) * CHUNK)
        for b in rows:
            o_inter[b].append(dot_nt(q_dec_b[b][sl], state[b].astype(BF16)))
        kv_t = [dot_tn(gv_b[b][sl], k_end[b][sl]) for b in rows]
        state = [state[b] * decay[b][n * CHUNK:n * CHUNK + 1, :] + jnp.where(head_diag, kv_t[b], 0.0)
                 for b in rows]
    o = [o[b] + jnp.concatenate(o_inter[b], axis=0) for b in rows]

    emit_qkv()
    hr = lax.broadcasted_iota(jnp.int32, (GLA_WIDTH, GLA_WIDTH), 0)
    hc = lax.broadcasted_iota(jnp.int32, (GLA_WIDTH, GLA_WIDTH), 1)
    ones_blk = jnp.where((hr // GLA_VDIM) == (hc // GLA_VDIM), 1.0, 0.0).astype(BF16)
    sq_split = [_split_bf16(v * v) for v in o]
    ms = [(dot(hi, ones_blk) + dot(lo, ones_blk)) * (1.0 / GLA_VDIM) for hi, lo in sq_split]
    o = [v * lax.rsqrt(m + EPS) * gng_ref[...] for v, m in zip(o, ms)]
    silu = [g / (1.0 + jnp.exp(-g)) for g in gr]

    emit_qkv(QKV_W // QKV_TILE)
    for b in rows:
        out_ref[b] = jnp.concatenate([y_pool[b], o[b] * silu[b]], axis=1).astype(out_ref.dtype)
        state_scr[b] = state[b]
        halo_scr[b, 0:POOL_HALO, :] = x[b][ts - POOL_HALO:ts, :]


def _inproj_mix(h, g, w, wg2, bg, gng, pw, ps, batch, seq, ts):
    t = batch * seq
    nb = math.gcd(batch, MIX_BATCH_ROWS)
    qkv, y_pg = pl.pallas_call(
        functools.partial(_inproj_mix_kernel, ts=ts, nb=nb),
        grid=(batch // nb, seq // ts),
        in_specs=[
            pl.BlockSpec((nb, ts, D_MODEL), lambda b, s: (b, s, 0)),
            _resident((1, D_MODEL)),
            _resident((D_MODEL, W_IN_COLS)),
            _resident((LANES, GLA_KW)),
            _resident((1, GLA_KW)),
            _resident((1, GLA_WIDTH)),
            _resident((POOL_WIDTH, POOL_WIDTH)),
            _resident((1, POOL_WIDTH)),
        ],
        out_specs=[
            pl.BlockSpec((nb, ts, QKV_W), lambda b, s: (b, s, 0)),
            pl.BlockSpec((nb, ts, POOL_WIDTH + GLA_WIDTH), lambda b, s: (b, s, 0)),
        ],
        out_shape=[
            jax.ShapeDtypeStruct((batch, seq, QKV_W), BF16),
            jax.ShapeDtypeStruct((batch, seq, POOL_WIDTH + GLA_WIDTH), BF16),
        ],
        scratch_shapes=[
            pltpu.VMEM((nb, GLA_WIDTH, GLA_KW), F32),
            pltpu.VMEM((nb, POOL_HALO + ts, POOL_WIDTH), F32),
        ],
        compiler_params=pltpu.CompilerParams(
            dimension_semantics=("parallel", "arbitrary"), vmem_limit_bytes=VMEM_LIMIT),
        name="inproj_mix",
    )(h.reshape(batch, seq, D_MODEL), g, w, wg2, bg, gng, pw, ps)
    return qkv.reshape(t, QKV_W), y_pg.reshape(t, POOL_WIDTH + GLA_WIDTH)


def _out_mlp_kernel(h_ref, ypg_ref, yd_ref, wo_ref, g2_ref, w1_ref, w2_ref, gf_ref, o_ref,
                    *, final):
    npg = POOL_WIDTH + GLA_WIDTH
    h1 = (h_ref[...]
          + jnp.dot(ypg_ref[...], wo_ref[0:npg, :], preferred_element_type=F32)
          + jnp.dot(yd_ref[...], wo_ref[npg:D_MODEL, :], preferred_element_type=F32))
    ms = jnp.mean(h1 * h1, axis=-1, keepdims=True)
    z = (h1 * lax.rsqrt(ms + EPS) * g2_ref[...]).astype(BF16)
    a = jnp.maximum(jnp.dot(z, w1_ref[...], preferred_element_type=F32), 0.0)
    acc = h1 + jnp.dot((a * a).astype(BF16), w2_ref[...], preferred_element_type=F32)
    if final:
        ms = jnp.mean(acc * acc, axis=-1, keepdims=True)
        acc = acc * lax.rsqrt(ms + EPS) * gf_ref[...]
    o_ref[...] = acc


def _out_mlp(h, ypg, yd, wo, g2, w1, w2, gf, tm, final):
    t = h.shape[0]
    return pl.pallas_call(
        functools.partial(_out_mlp_kernel, final=final),
        grid=(t // tm,),
        in_specs=[
            pl.BlockSpec((tm, D_MODEL), lambda i: (i, 0)),
            pl.BlockSpec((tm, POOL_WIDTH + GLA_WIDTH), lambda i: (i, 0)),
            pl.BlockSpec((tm, DIFF_WIDTH), lambda i: (i, 0)),
            _resident((D_MODEL, D_MODEL)),
            _resident((1, D_MODEL)),
            _resident((D_MODEL, D_FF)),
            _resident((D_FF, D_MODEL)),
            _resident((1, D_MODEL)),
        ],
        out_specs=pl.BlockSpec((tm, D_MODEL), lambda i: (i, 0)),
        out_shape=jax.ShapeDtypeStruct((t, D_MODEL), F32),
        compiler_params=pltpu.CompilerParams(
            dimension_semantics=("parallel",), vmem_limit_bytes=VMEM_LIMIT),
        name="out_mlp",
    )(h, ypg, yd, wo, g2, w1, w2, gf)


def _prep_w_in(w):
    pool = w[:, 0:256]
    dq = w[:, 256:768] * (DIFF_QKDIM ** -0.5 * LOG2_E)
    dk = w[:, 768:1280]
    dv = w[:, 1280:1792]
    gq = w[:, 1792:1920]
    gk = w[:, 1920:2048]
    gv = w[:, 2048:2304]
    gr = w[:, 2304:2560]
    gg = w[:, 2560:2576]
    pad = jnp.zeros((D_MODEL, REST_W - REST_GG - GLA_GATE_RANK), w.dtype)
    return jnp.concatenate([dq, dk, dv, pool, gq, gk, gv, gr, gg, pad], axis=1).astype(BF16)


def _prep_w_out(w):
    return jnp.concatenate([w[0:256], w[768:1024], w[256:768]], axis=0).astype(BF16)


def _prep_pool_w(pw):
    out = jnp.zeros((POOL_WIDTH, POOL_WIDTH), F32)
    for g in range(len(POOL_WINDOWS)):
        out = out.at[g * POOL_GDIM:(g + 1) * POOL_GDIM, g * POOL_GDIM:(g + 1) * POOL_GDIM].set(pw[g])
    return out.astype(BF16)


def _tiles(seq, t):
    tm = math.gcd(t, 512)
    tq = math.gcd(seq, 512)
    ts = math.gcd(seq, 256)
    return tm, tq, ts


def kernel(x, norm1_g, w_in, pool_w, pool_scale, diff_lq1, diff_lk1, diff_lq2, diff_lk2, diff_norm_g, gla_w_gate2, gla_b_gate, gla_norm_g, w_out, norm2_g, w_mlp1, w_mlp2, final_norm_g):
    batch, seq, d = x.shape
    assert d == D_MODEL and seq % CHUNK == 0
    depth = w_in.shape[0]
    t = batch * seq
    tm, tq, ts = _tiles(seq, t)
    h = x.reshape(t, D_MODEL)
    gf = final_norm_g.reshape(1, D_MODEL)
    for l in range(depth):
        lambda_init = 0.8 - 0.6 * math.exp(-0.3 * l)
        wg2 = jnp.zeros((LANES, GLA_KW), F32).at[0:GLA_GATE_RANK].set(gla_w_gate2[l])
        qkv, y_pg = _inproj_mix(h, norm1_g[l].reshape(1, D_MODEL), _prep_w_in(w_in[l]), wg2,
                                gla_b_gate[l].reshape(1, GLA_KW),
                                gla_norm_g[l].reshape(1, GLA_WIDTH), _prep_pool_w(pool_w[l]),
                                pool_scale[l].reshape(1, POOL_WIDTH), batch, seq, ts)
        lam = _diff_lambda(diff_lq1[l], diff_lk1[l], diff_lq2[l], diff_lk2[l], lambda_init)
        y_diff = _diff_attention(qkv, lam, diff_norm_g[l].reshape(1, DIFF_WIDTH),
                                 batch, seq, tq, lambda_init)
        h = _out_mlp(h, y_pg, y_diff, _prep_w_out(w_out[l]), norm2_g[l].reshape(1, D_MODEL),
                     w_mlp1[l].astype(BF16), w_mlp2[l].astype(BF16), gf, tm,
                     final=(l == depth - 1))
    return h.reshape(batch, seq, D_MODEL)
```

```python
import functools
import math

import jax
import jax.numpy as jnp
from jax import lax
from jax.experimental import pallas as pl
from jax.experimental.pallas import tpu as pltpu

F32 = jnp.float32
BF16 = jnp.bfloat16

D_MODEL = 1024
CHUNK = 64
POOL_WIDTH = 256
POOL_WINDOWS = (2, 4, 8, 16)
POOL_GDIM = 64
POOL_HALO = 16
DIFF_HEADS = 4
DIFF_VDIM = 128
DIFF_QKDIM = 64
DIFF_WIDTH = 512
GLA_HEADS = 4
GLA_VDIM = 64
GLA_KDIM = 32
GLA_WIDTH = 256
GLA_KW = GLA_HEADS * GLA_KDIM
GLA_GATE_RANK = 16
GLA_GATE_TAU = 16.0
D_FF = 4096
EPS = 1e-6

LANES = 128
QKV_W = 3 * DIFF_WIDTH
REST_POOL = 0
REST_GQ = 256
REST_GK = 384
REST_GV = 512
REST_GR = 768
REST_GG = 1024
REST_W = 1152
W_IN_COLS = QKV_W + REST_W

VMEM_LIMIT = 56 * 1024 * 1024
NEG_BIG = -1e30
LOG2_E = 1.4426950408889634
ATTN_ROW_BLOCK = 256
ATTN_PAIRS_PER_TRIP = 8
MIX_BATCH_ROWS = 4
QKV_TILE = 256


def _resident(shape):
    nd = len(shape)
    return pl.BlockSpec(shape, lambda *_: (0,) * nd, pipeline_mode=pl.Buffered(1))


def _split_bf16(a):
    hi = a.astype(BF16)
    lo = (a - hi.astype(F32)).astype(BF16)
    return hi, lo


def _attn_kernel(q_ref, k_ref, v_ref, lam_ref, g_ref, o_ref,
                 e_scr, kx_scr, sa_scr, sb_scr, la_scr, lb_scr, m_scr, acc_scr,
                 *, tq, rb, nq, lambda_init):
    m_rows = 2 * tq
    nblk = m_rows // rb
    reps = tq // LANES
    nitems = nq * (nq + 1) // 2

    @pl.when(jnp.logical_and(pl.program_id(0) == 0, pl.program_id(1) == 0))
    def _():
        kchunk = lax.broadcasted_iota(jnp.int32, (tq, LANES), 0) // CHUNK
        klane = lax.broadcasted_iota(jnp.int32, (tq, LANES), 1)
        kx_scr[...] = jnp.where(kchunk > klane, 1.0, 0.0).astype(BF16)
        qchunk = (lax.broadcasted_iota(jnp.int32, (m_rows, LANES), 0) % tq) // CHUNK
        qlane = lax.broadcasted_iota(jnp.int32, (m_rows, LANES), 1)
        e_scr[0] = jnp.zeros((m_rows, LANES), BF16)
        e_scr[1] = jnp.where(qchunk == qlane, NEG_BIG, 0.0).astype(BF16)
        acc_scr[...] = jnp.zeros(acc_scr.shape, F32)

    lane = lax.broadcasted_iota(jnp.int32, (rb, LANES), 1)

    def qk_block(item, s_dst, r):
        qi, j = item
        rows = slice(r * rb, (r + 1) * rb)
        k = jnp.concatenate([k_ref[pl.ds(pl.multiple_of(j * tq, tq), tq), :], kx_scr[...]],
                            axis=1)
        second_map, row0 = divmod(r * rb, tq)
        q_raw = q_ref[pl.ds(pl.multiple_of(qi * tq + row0, rb), rb), :]
        keep = (lane >= DIFF_QKDIM) if second_map else (lane < DIFF_QKDIM)
        q_half = jnp.where(keep, q_raw, jnp.zeros_like(q_raw))
        q = jnp.concatenate([q_half, e_scr[(j == qi).astype(jnp.int32), rows, :]], axis=1)
        s = lax.dot_general(q, k, (((1,), (1,)), ((), ())), preferred_element_type=F32)
        s_dst[0][rows, :] = s
        lane_max = s[:, 0:LANES]
        for c in range(1, reps):
            lane_max = jnp.maximum(lane_max, s[:, c * LANES:(c + 1) * LANES])
        s_dst[1][rows, :] = lane_max

    def softmax_pv_block(item, s_src, r):
        qi, j = item
        rows = slice(r * rb, (r + 1) * rb)
        m_prev = jnp.where(j == 0, -jnp.inf, m_scr[rows, :])
        m_next = jnp.maximum(m_prev, jnp.max(s_src[1][rows, :], axis=1, keepdims=True))
        alpha = jnp.exp2(m_prev - m_next)
        p = jnp.exp2(s_src[0][rows, :] - jnp.concatenate([m_next] * reps, axis=1))
        m_scr[rows, :] = m_next
        va = jnp.concatenate([v_ref[pl.ds(pl.multiple_of(j * tq, tq), tq), :],
                              jnp.ones((tq, DIFF_VDIM), BF16)], axis=1)
        acc_scr[qi, rows, :] = (jnp.concatenate([alpha, alpha], axis=1) * acc_scr[qi, rows, :]
                                + jnp.dot(p.astype(BF16), va, preferred_element_type=F32))

    def stage(item_next, s_next, item_cur, s_cur):
        for r in range(nblk):
            if item_next is not None:
                qk_block(item_next, s_next, r)
            softmax_pv_block(item_cur, s_cur, r)

    def following(item):
        qi, j = item
        last = j == qi
        qn = jnp.where(last, qi + 1, qi)
        jn = jnp.where(last, 0, j + 1)
        over = qn >= nq
        return jnp.where(over, qi, qn), jnp.where(over, j, jn)

    first = (jnp.int32(0), jnp.int32(0))
    buf_a = (sa_scr, la_scr)
    buf_b = (sb_scr, lb_scr)
    for r in range(nblk):
        qk_block(first, buf_a, r)

    def pair(item0):
        item1 = following(item0)
        item2 = following(item1)
        stage(item1, buf_b, item0, buf_a)
        stage(item2, buf_a, item1, buf_b)
        return item2

    def body(_, item):
        for _ in range(ATTN_PAIRS_PER_TRIP):
            item = pair(item)
        return item

    npairs = nitems // 2
    item = lax.fori_loop(0, npairs // ATTN_PAIRS_PER_TRIP, body, first)
    for _ in range(npairs % ATTN_PAIRS_PER_TRIP):
        item = pair(item)
    if nitems % 2:
        stage(None, None, item, buf_a)

    def finalize(i, carry):
        acc = acc_scr[i]
        a1, l1 = acc[0:tq, 0:DIFF_VDIM], acc[0:tq, DIFF_VDIM:2 * DIFF_VDIM]
        a2, l2 = acc[tq:m_rows, 0:DIFF_VDIM], acc[tq:m_rows, DIFF_VDIM:2 * DIFF_VDIM]
        o = (a1 * l2 - lam_ref[...] * (a2 * l1)) / (l1 * l2)
        ms = jnp.mean(o * o, axis=-1, keepdims=True)
        o = o * lax.rsqrt(ms + EPS) * g_ref[...] * (1.0 - lambda_init)
        o_ref[pl.ds(pl.multiple_of(i * tq, tq), tq), :] = o.astype(o_ref.dtype)
        return carry

    lax.fori_loop(0, nq, finalize, 0, unroll=math.gcd(nq, 4))


def _lambda_kernel(lq1_ref, lk1_ref, lq2_ref, lk2_ref, lam_ref, *, lambda_init):
    a = jnp.sum(lq1_ref[...] * lk1_ref[...], axis=-1, keepdims=True)
    b = jnp.sum(lq2_ref[...] * lk2_ref[...], axis=-1, keepdims=True)
    lam_ref[...] = jnp.broadcast_to(jnp.exp(a) - jnp.exp(b) + lambda_init, lam_ref.shape)


def _diff_lambda(lq1, lk1, lq2, lk2, lambda_init):
    args = [a.reshape(1, DIFF_QKDIM) for a in (lq1, lk1, lq2, lk2)]
    return pl.pallas_call(
        functools.partial(_lambda_kernel, lambda_init=lambda_init),
        out_shape=jax.ShapeDtypeStruct((1, LANES), F32),
        name="diff_lambda",
    )(*args)


def _diff_attention(qkv, lam, norm_g, batch, seq, tq, lambda_init):
    t = batch * seq
    nq = seq // tq
    kcol = DIFF_WIDTH // DIFF_VDIM
    vcol = 2 * DIFF_WIDTH // DIFF_VDIM
    return pl.pallas_call(
        functools.partial(_attn_kernel, tq=tq, rb=math.gcd(ATTN_ROW_BLOCK, tq), nq=nq,
                          lambda_init=lambda_init),
        grid=(batch, DIFF_HEADS),
        in_specs=[
            pl.BlockSpec((seq, DIFF_VDIM), lambda b, h: (b, h)),
            pl.BlockSpec((seq, DIFF_VDIM), lambda b, h: (b, kcol + h)),
            pl.BlockSpec((seq, DIFF_VDIM), lambda b, h: (b, vcol + h)),
            pl.BlockSpec((1, LANES), lambda b, h: (0, 0)),
            pl.BlockSpec((1, DIFF_VDIM), lambda b, h: (0, h)),
        ],
        out_specs=pl.BlockSpec((seq, DIFF_VDIM), lambda b, h: (b, h)),
        out_shape=jax.ShapeDtypeStruct((t, DIFF_WIDTH), BF16),
        scratch_shapes=[
            pltpu.VMEM((2, 2 * tq, LANES), BF16),
            pltpu.VMEM((tq, LANES), BF16),
            pltpu.VMEM((2 * tq, tq), F32),
            pltpu.VMEM((2 * tq, tq), F32),
            pltpu.VMEM((2 * tq, LANES), F32),
            pltpu.VMEM((2 * tq, LANES), F32),
            pltpu.VMEM((2 * tq, LANES), F32),
            pltpu.VMEM((nq, 2 * tq, 2 * DIFF_VDIM), F32),
        ],
        compiler_params=pltpu.CompilerParams(
            dimension_semantics=("arbitrary", "arbitrary"), vmem_limit_bytes=VMEM_LIMIT),
        name="diff_attention",
    )(qkv, qkv, qkv, lam, norm_g)


def _inproj_mix_kernel(h_ref, g_ref, w_ref, wg2_ref, bg_ref, gng_ref, pw_ref, ps_ref,
                       qkv_ref, out_ref, state_scr, halo_scr, *, ts, nb):
    st = pl.program_id(1)
    nchunk = ts // CHUNK
    rows = range(nb)

    @pl.when(st == 0)
    def _():
        state_scr[...] = jnp.zeros_like(state_scr)
        halo_scr[:, 0:POOL_HALO, :] = jnp.zeros((nb, POOL_HALO, POOL_WIDTH), F32)

    def dot(a, b):
        return jnp.dot(a, b, preferred_element_type=F32)

    def dot_nt(a, b):
        return lax.dot_general(a, b, (((1,), (1,)), ((), ())), preferred_element_type=F32)

    def dot_tn(a, b):
        return lax.dot_general(a, b, (((0,), (0,)), ((), ())), preferred_element_type=F32)

    h = h_ref[...].reshape(nb * ts, D_MODEL)
    ms = jnp.mean(h * h, axis=-1, keepdims=True)
    u = (h * lax.rsqrt(ms + EPS) * g_ref[...]).astype(BF16)
    rest_all = dot(u, w_ref[:, QKV_W:])
    rest = [rest_all[b * ts:(b + 1) * ts] for b in rows]

    qkv_tiles = iter(range(QKV_W // QKV_TILE))

    def emit_qkv(count=1):
        for _ in range(count):
            c = next(qkv_tiles, None)
            if c is not None:
                cols = slice(c * QKV_TILE, (c + 1) * QKV_TILE)
                y = dot(u, w_ref[:, cols]).astype(BF16)
                for b in rows:
                    qkv_ref[b, :, cols] = y[b * ts:(b + 1) * ts]

    x = [rest[b][:, REST_POOL:REST_POOL + POOL_WIDTH] for b in rows]
    for b in rows:
        halo_scr[b, POOL_HALO:POOL_HALO + ts, :] = x[b]

    def window_sums(b):
        xf = halo_scr[b]
        xa = xf[:, 0:LANES]
        c2 = xa + pltpu.roll(xa, 1, axis=0)
        c4 = c2 + pltpu.roll(c2, 2, axis=0)
        xb = xf[:, LANES:2 * LANES]
        d2 = xb + pltpu.roll(xb, 1, axis=0)
        d4 = d2 + pltpu.roll(d2, 2, axis=0)
        c8 = d4 + pltpu.roll(d4, 4, axis=0)
        c16 = c8 + pltpu.roll(c8, 8, axis=0)
        cut = slice(POOL_HALO, POOL_HALO + ts)
        return xa[cut], xb[cut], c2[cut], c4[cut], c8[cut], c16[cut]

    sums = [window_sums(b) for b in rows]
    tpos = (st * ts + lax.broadcasted_iota(jnp.int32, (ts, LANES), 0) + 1).astype(F32)
    lane = lax.broadcasted_iota(jnp.int32, (ts, LANES), 1)
    first = lane < POOL_GDIM
    inv_a = jnp.where(first, 1.0 / jnp.minimum(tpos, 2.0), 1.0 / jnp.minimum(tpos, 4.0))
    inv_b = jnp.where(first, 1.0 / jnp.minimum(tpos, 8.0), 1.0 / jnp.minimum(tpos, 16.0))

    def pooled_of(xa, xb, c2, c4, c8, c16):
        mean_a = jnp.where(first, c2, c4) * inv_a
        mean_b = jnp.where(first, c8, c16) * inv_b
        return jnp.concatenate([mean_a - xa, mean_b - xb], axis=1).astype(BF16)

    pooled = [pooled_of(*s) for s in sums]
    pw = pw_ref[...]
    y_pool = [dot(p, pw) * ps_ref[...] for p in pooled]

    emit_qkv()
    gq = [rest[b][:, REST_GQ:REST_GQ + GLA_KW] for b in rows]
    gk = [rest[b][:, REST_GK:REST_GK + GLA_KW] for b in rows]
    gv_b = [rest[b][:, REST_GV:REST_GV + GLA_WIDTH].astype(BF16) for b in rows]
    gr = [rest[b][:, REST_GR:REST_GR + GLA_WIDTH] for b in rows]
    gg = [rest[b][:, REST_GG:REST_GG + LANES] for b in rows]

    w_hi, w_lo = _split_bf16(wg2_ref[...])
    w_cat = jnp.concatenate([w_hi, w_lo], axis=1)
    g_split = [_split_bf16(v) for v in gg]
    xg = []
    for hi, lo in g_split:
        hh = dot(hi, w_cat)
        xg.append(hh[:, 0:GLA_KW] + hh[:, GLA_KW:2 * GLA_KW] + dot(lo, w_hi) + bg_ref[...])
    log_a = [(jnp.minimum(v, 0.0) - jnp.log(1.0 + jnp.exp(-jnp.abs(v)))) * (1.0 / GLA_GATE_TAU)
             for v in xg]

    emit_qkv()
    r = lax.broadcasted_iota(jnp.int32, (ts, ts), 0)
    c = lax.broadcasted_iota(jnp.int32, (ts, ts), 1)
    same_chunk = (r // CHUNK) == (c // CHUNK)
    causal = jnp.logical_and(same_chunk, c <= r)
    tri = jnp.where(causal, 1.0, 0.0).astype(BF16)
    blk = jnp.where(same_chunk, 1.0, 0.0).astype(BF16)
    la_cat = [jnp.concatenate(_split_bf16(v), axis=1) for v in log_a]
    bcum2 = [dot(tri, v) for v in la_cat]
    btot2 = [dot(blk, v) for v in la_cat]
    bcum = [v[:, 0:GLA_KW] + v[:, GLA_KW:2 * GLA_KW] for v in bcum2]
    btot = [v[:, 0:GLA_KW] + v[:, GLA_KW:2 * GLA_KW] for v in btot2]

    q_dec_b = [((q * (GLA_KDIM ** -0.5)) * jnp.exp(bc)).astype(BF16) for q, bc in zip(gq, bcum)]
    k_inv = [(k * jnp.exp(-bc)).astype(BF16) for k, bc in zip(gk, bcum)]
    k_end = [(k * jnp.exp(bt - bc)).astype(BF16) for k, bt, bc in zip(gk, btot, bcum)]
    decay = [jnp.exp(bt) for bt in btot]

    emit_qkv()
    klane = lax.broadcasted_iota(jnp.int32, (ts, GLA_KW), 1)
    vlane = lax.broadcasted_iota(jnp.int32, (ts, GLA_WIDTH), 1)
    o = [None] * nb
    for h in range(GLA_HEADS):
        qh = [jnp.where(klane // GLA_KDIM == h, q, jnp.zeros_like(q)) for q in q_dec_b]
        att = [jnp.where(causal, dot_nt(q, k), 0.0).astype(BF16) for q, k in zip(qh, k_inv)]
        vh = [jnp.where(vlane // GLA_VDIM == h, v, jnp.zeros_like(v)) for v in gv_b]
        part = [dot(a, v) for a, v in zip(att, vh)]
        o = part if h == 0 else [a + b for a, b in zip(o, part)]
        if h % 2 == 1:
            emit_qkv()

    srow = lax.broadcasted_iota(jnp.int32, (GLA_WIDTH, GLA_KW), 0)
    scol = lax.broadcasted_iota(jnp.int32, (GLA_WIDTH, GLA_KW), 1)
    head_diag = (srow // GLA_VDIM) == (scol // GLA_KDIM)
    state = [state_scr[b] for b in rows]
    o_inter = [[] for _ in rows]
    for n in range(nchunk):
        sl = slice(n * CHUNK, (n + 1) * CHUNK)
        for b in rows:
            o_inter[b].append(dot_nt(q_dec_b[b][sl], state[b].astype(BF16)))
        kv_t = [dot_tn(gv_b[b][sl], k_end[b][sl]) for b in rows]
        state = [state[b] * decay[b][n * CHUNK:n * CHUNK + 1, :] + jnp.where(head_diag, kv_t[b], 0.0)
                 for b in rows]
    o = [o[b] + jnp.concatenate(o_inter[b], axis=0) for b in rows]

    emit_qkv()
    hr = lax.broadcasted_iota(jnp.int32, (GLA_WIDTH, GLA_WIDTH), 0)
    hc = lax.broadcasted_iota(jnp.int32, (GLA_WIDTH, GLA_WIDTH), 1)
    ones_blk = jnp.where((hr // GLA_VDIM) == (hc // GLA_VDIM), 1.0, 0.0).astype(BF16)
    sq_split = [_split_bf16(v * v) for v in o]
    ms = [(dot(hi, ones_blk) + dot(lo, ones_blk)) * (1.0 / GLA_VDIM) for hi, lo in sq_split]
    o = [v * lax.rsqrt(m + EPS) * gng_ref[...] for v, m in zip(o, ms)]
    silu = [g / (1.0 + jnp.exp(-g)) for g in gr]

    emit_qkv(QKV_W // QKV_TILE)
    for b in rows:
        out_ref[b] = jnp.concatenate([y_pool[b], o[b] * silu[b]], axis=1).astype(out_ref.dtype)
        state_scr[b] = state[b]
        halo_scr[b, 0:POOL_HALO, :] = x[b][ts - POOL_HALO:ts, :]


def _inproj_mix(h, g, w, wg2, bg, gng, pw, ps, batch, seq, ts):
    t = batch * seq
    nb = math.gcd(batch, MIX_BATCH_ROWS)
    qkv, y_pg = pl.pallas_call(
        functools.partial(_inproj_mix_kernel, ts=ts, nb=nb),
        grid=(batch // nb, seq // ts),
        in_specs=[
            pl.BlockSpec((nb, ts, D_MODEL), lambda b, s: (b, s, 0)),
            _resident((1, D_MODEL)),
            _resident((D_MODEL, W_IN_COLS)),
            _resident((LANES, GLA_KW)),
            _resident((1, GLA_KW)),
            _resident((1, GLA_WIDTH)),
            _resident((POOL_WIDTH, POOL_WIDTH)),
            _resident((1, POOL_WIDTH)),
        ],
        out_specs=[
            pl.BlockSpec((nb, ts, QKV_W), lambda b, s: (b, s, 0)),
            pl.BlockSpec((nb, ts, POOL_WIDTH + GLA_WIDTH), lambda b, s: (b, s, 0)),
        ],
        out_shape=[
            jax.ShapeDtypeStruct((batch, seq, QKV_W), BF16),
            jax.ShapeDtypeStruct((batch, seq, POOL_WIDTH + GLA_WIDTH), BF16),
        ],
        scratch_shapes=[
            pltpu.VMEM((nb, GLA_WIDTH, GLA_KW), F32),
            pltpu.VMEM((nb, POOL_HALO + ts, POOL_WIDTH), F32),
        ],
        compiler_params=pltpu.CompilerParams(
            dimension_semantics=("parallel", "arbitrary"), vmem_limit_bytes=VMEM_LIMIT),
        name="inproj_mix",
    )(h.reshape(batch, seq, D_MODEL), g, w, wg2, bg, gng, pw, ps)
    return qkv.reshape(t, QKV_W), y_pg.reshape(t, POOL_WIDTH + GLA_WIDTH)


def _out_mlp_kernel(h_ref, ypg_ref, yd_ref, wo_ref, g2_ref, w1_ref, w2_ref, gf_ref, o_ref,
                    *, final):
    npg = POOL_WIDTH + GLA_WIDTH
    h1 = (h_ref[...]
          + jnp.dot(ypg_ref[...], wo_ref[0:npg, :], preferred_element_type=F32)
          + jnp.dot(yd_ref[...], wo_ref[npg:D_MODEL, :], preferred_element_type=F32))
    ms = jnp.mean(h1 * h1, axis=-1, keepdims=True)
    z = (h1 * lax.rsqrt(ms + EPS) * g2_ref[...]).astype(BF16)
    a = jnp.maximum(jnp.dot(z, w1_ref[...], preferred_element_type=F32), 0.0)
    acc = h1 + jnp.dot((a * a).astype(BF16), w2_ref[...], preferred_element_type=F32)
    if final:
        ms = jnp.mean(acc * acc, axis=-1, keepdims=True)
        acc = acc * lax.rsqrt(ms + EPS) * gf_ref[...]
    o_ref[...] = acc


def _out_mlp(h, ypg, yd, wo, g2, w1, w2, gf, tm, final):
    t = h.shape[0]
    return pl.pallas_call(
        functools.partial(_out_mlp_kernel, final=final),
        grid=(t // tm,),
        in_specs=[
            pl.BlockSpec((tm, D_MODEL), lambda i: (i, 0)),
            pl.BlockSpec((tm, POOL_WIDTH + GLA_WIDTH), lambda i: (i, 0)),
            pl.BlockSpec((tm, DIFF_WIDTH), lambda i: (i, 0)),
            _resident((D_MODEL, D_MODEL)),
            _resident((1, D_MODEL)),
            _resident((D_MODEL, D_FF)),
            _resident((D_FF, D_MODEL)),
            _resident((1, D_MODEL)),
        ],
        out_specs=pl.BlockSpec((tm, D_MODEL), lambda i: (i, 0)),
        out_shape=jax.ShapeDtypeStruct((t, D_MODEL), F32),
        compiler_params=pltpu.CompilerParams(
            dimension_semantics=("parallel",), vmem_limit_bytes=VMEM_LIMIT),
        name="out_mlp",
    )(h, ypg, yd, wo, g2, w1, w2, gf)


def _prep_w_in(w):
    pool = w[:, 0:256]
    dq = w[:, 256:768] * (DIFF_QKDIM ** -0.5 * LOG2_E)
    dk = w[:, 768:1280]
    dv = w[:, 1280:1792]
    gq = w[:, 1792:1920]
    gk = w[:, 1920:2048]
    gv = w[:, 2048:2304]
    gr = w[:, 2304:2560]
    gg = w[:, 2560:2576]
    pad = jnp.zeros((D_MODEL, REST_W - REST_GG - GLA_GATE_RANK), w.dtype)
    return jnp.concatenate([dq, dk, dv, pool, gq, gk, gv, gr, gg, pad], axis=1).astype(BF16)


def _prep_w_out(w):
    return jnp.concatenate([w[0:256], w[768:1024], w[256:768]], axis=0).astype(BF16)


def _prep_pool_w(pw):
    out = jnp.zeros((POOL_WIDTH, POOL_WIDTH), F32)
    for g in range(len(POOL_WINDOWS)):
        out = out.at[g * POOL_GDIM:(g + 1) * POOL_GDIM, g * POOL_GDIM:(g + 1) * POOL_GDIM].set(pw[g])
    return out.astype(BF16)


def _tiles(seq, t):
    tm = math.gcd(t, 512)
    tq = math.gcd(seq, 512)
    ts = math.gcd(seq, 256)
    return tm, tq, ts


def kernel(x, norm1_g, w_in, pool_w, pool_scale, diff_lq1, diff_lk1, diff_lq2, diff_lk2, diff_norm_g, gla_w_gate2, gla_b_gate, gla_norm_g, w_out, norm2_g, w_mlp1, w_mlp2, final_norm_g):
    batch, seq, d = x.shape
    assert d == D_MODEL and seq % CHUNK == 0
    depth = w_in.shape[0]
    t = batch * seq
    tm, tq, ts = _tiles(seq, t)
    h = x.reshape(t, D_MODEL)
    gf = final_norm_g.reshape(1, D_MODEL)
    for l in range(depth):
        lambda_init = 0.8 - 0.6 * math.exp(-0.3 * l)
        wg2 = jnp.zeros((LANES, GLA_KW), F32).at[0:GLA_GATE_RANK].set(gla_w_gate2[l])
        qkv, y_pg = _inproj_mix(h, norm1_g[l].reshape(1, D_MODEL), _prep_w_in(w_in[l]), wg2,
                                gla_b_gate[l].reshape(1, GLA_KW),
                                gla_norm_g[l].reshape(1, GLA_WIDTH), _prep_pool_w(pool_w[l]),
                                pool_scale[l].reshape(1, POOL_WIDTH), batch, seq, ts)
        lam = _diff_lambda(diff_lq1[l], diff_lk1[l], diff_lq2[l], diff_lk2[l], lambda_init)
        y_diff = _diff_attention(qkv, lam, diff_norm_g[l].reshape(1, DIFF_WIDTH),
                                 batch, seq, tq, lambda_init)
        h = _out_mlp(h, y_pg, y_diff, _prep_w_out(w_out[l]), norm2_g[l].reshape(1, D_MODEL),
                     w_mlp1[l].astype(BF16), w_mlp2[l].astype(BF16), gf, tm,
                     final=(l == depth - 1))
    return h.reshape(batch, seq, D_MODEL)
```

```python
import functools
import math

import jax
import jax.numpy as jnp
from jax import lax
from jax.experimental import pallas as pl
from jax.experimental.pallas import tpu as pltpu

F32 = jnp.float32
BF16 = jnp.bfloat16

D_MODEL = 1024
CHUNK = 64
POOL_WIDTH = 256
POOL_WINDOWS = (2, 4, 8, 16)
POOL_GDIM = 64
POOL_HALO = 16
DIFF_HEADS = 4
DIFF_VDIM = 128
DIFF_QKDIM = 64
DIFF_WIDTH = 512
GLA_HEADS = 4
GLA_VDIM = 64
GLA_KDIM = 32
GLA_WIDTH = 256
GLA_KW = GLA_HEADS * GLA_KDIM
GLA_GATE_RANK = 16
GLA_GATE_TAU = 16.0
D_FF = 4096
EPS = 1e-6

LANES = 128
QKV_W = 3 * DIFF_WIDTH
REST_POOL = 0
REST_GQ = 256
REST_GK = 384
REST_GV = 512
REST_GR = 768
REST_GG = 1024
REST_W = 1152
W_IN_COLS = QKV_W + REST_W

VMEM_LIMIT = 56 * 1024 * 1024
NEG_BIG = -1e30
LOG2_E = 1.4426950408889634
ATTN_ROW_BLOCK = 256
ATTN_PAIRS_PER_TRIP = 8
MIX_BATCH_ROWS = 4
QKV_TILE = 256


def _resident(shape):
    nd = len(shape)
    return pl.BlockSpec(shape, lambda *_: (0,) * nd, pipeline_mode=pl.Buffered(1))


def _split_bf16(a):
    hi = a.astype(BF16)
    lo = (a - hi.astype(F32)).astype(BF16)
    return hi, lo


def _attn_kernel(q_ref, k_ref, v_ref, lq1_ref, lk1_ref, lq2_ref, lk2_ref, g_ref, o_ref,
                 e_scr, kx_scr, sa_scr, sb_scr, la_scr, lb_scr, m_scr, acc_scr,
                 *, tq, rb, nq, lambda_init):
    m_rows = 2 * tq
    nblk = m_rows // rb
    reps = tq // LANES
    nitems = nq * (nq + 1) // 2

    @pl.when(jnp.logical_and(pl.program_id(0) == 0, pl.program_id(1) == 0))
    def _():
        kchunk = lax.broadcasted_iota(jnp.int32, (tq, LANES), 0) // CHUNK
        klane = lax.broadcasted_iota(jnp.int32, (tq, LANES), 1)
        kx_scr[...] = jnp.where(kchunk > klane, 1.0, 0.0).astype(BF16)
        qchunk = (lax.broadcasted_iota(jnp.int32, (m_rows, LANES), 0) % tq) // CHUNK
        qlane = lax.broadcasted_iota(jnp.int32, (m_rows, LANES), 1)
        e_scr[0] = jnp.zeros((m_rows, LANES), BF16)
        e_scr[1] = jnp.where(qchunk == qlane, NEG_BIG, 0.0).astype(BF16)
        acc_scr[...] = jnp.zeros(acc_scr.shape, F32)

    lane = lax.broadcasted_iota(jnp.int32, (rb, LANES), 1)

    def qk_block(item, s_dst, r):
        qi, j = item
        rows = slice(r * rb, (r + 1) * rb)
        k = jnp.concatenate([k_ref[pl.ds(pl.multiple_of(j * tq, tq), tq), :], kx_scr[...]],
                            axis=1)
        second_map, row0 = divmod(r * rb, tq)
        q_raw = q_ref[pl.ds(pl.multiple_of(qi * tq + row0, rb), rb), :]
        keep = (lane >= DIFF_QKDIM) if second_map else (lane < DIFF_QKDIM)
        q_half = jnp.where(keep, q_raw, jnp.zeros_like(q_raw))
        q = jnp.concatenate([q_half, e_scr[(j == qi).astype(jnp.int32), rows, :]], axis=1)
        s = lax.dot_general(q, k, (((1,), (1,)), ((), ())), preferred_element_type=F32)
        s_dst[0][rows, :] = s
        lane_max = s[:, 0:LANES]
        for c in range(1, reps):
            lane_max = jnp.maximum(lane_max, s[:, c * LANES:(c + 1) * LANES])
        s_dst[1][rows, :] = lane_max

    def softmax_pv_block(item, s_src, r):
        qi, j = item
        rows = slice(r * rb, (r + 1) * rb)
        m_prev = jnp.where(j == 0, -jnp.inf, m_scr[rows, :])
        m_next = jnp.maximum(m_prev, jnp.max(s_src[1][rows, :], axis=1, keepdims=True))
        alpha = jnp.exp2(m_prev - m_next)
        p = jnp.exp2(s_src[0][rows, :] - jnp.concatenate([m_next] * reps, axis=1))
        m_scr[rows, :] = m_next
        va = jnp.concatenate([v_ref[pl.ds(pl.multiple_of(j * tq, tq), tq), :],
                              jnp.ones((tq, DIFF_VDIM), BF16)], axis=1)
        acc_scr[qi, rows, :] = (jnp.concatenate([alpha, alpha], axis=1) * acc_scr[qi, rows, :]
                                + jnp.dot(p.astype(BF16), va, preferred_element_type=F32))

    def stage(item_next, s_next, item_cur, s_cur):
        for r in range(nblk):
            if item_next is not None:
                qk_block(item_next, s_next, r)
            softmax_pv_block(item_cur, s_cur, r)

    def following(item):
        qi, j = item
        last = j == qi
        qn = jnp.where(last, qi + 1, qi)
        jn = jnp.where(last, 0, j + 1)
        over = qn >= nq
        return jnp.where(over, qi, qn), jnp.where(over, j, jn)

    first = (jnp.int32(0), jnp.int32(0))
    buf_a = (sa_scr, la_scr)
    buf_b = (sb_scr, lb_scr)
    for r in range(nblk):
        qk_block(first, buf_a, r)

    def pair(item0):
        item1 = following(item0)
        item2 = following(item1)
        stage(item1, buf_b, item0, buf_a)
        stage(item2, buf_a, item1, buf_b)
        return item2

    def body(_, item):
        for _ in range(ATTN_PAIRS_PER_TRIP):
            item = pair(item)
        return item

    npairs = nitems // 2
    item = lax.fori_loop(0, npairs // ATTN_PAIRS_PER_TRIP, body, first)
    for _ in range(npairs % ATTN_PAIRS_PER_TRIP):
        item = pair(item)
    if nitems % 2:
        stage(None, None, item, buf_a)

    lam_a = jnp.sum(lq1_ref[...] * lk1_ref[...], axis=-1, keepdims=True)
    lam_b = jnp.sum(lq2_ref[...] * lk2_ref[...], axis=-1, keepdims=True)
    lam = jnp.broadcast_to(jnp.exp(lam_a) - jnp.exp(lam_b) + lambda_init, (1, LANES))

    def finalize(i, carry):
        acc = acc_scr[i]
        a1, l1 = acc[0:tq, 0:DIFF_VDIM], acc[0:tq, DIFF_VDIM:2 * DIFF_VDIM]
        a2, l2 = acc[tq:m_rows, 0:DIFF_VDIM], acc[tq:m_rows, DIFF_VDIM:2 * DIFF_VDIM]
        o = (a1 * l2 - lam * (a2 * l1)) / (l1 * l2)
        ms = jnp.mean(o * o, axis=-1, keepdims=True)
        o = o * lax.rsqrt(ms + EPS) * g_ref[...] * (1.0 - lambda_init)
        o_ref[pl.ds(pl.multiple_of(i * tq, tq), tq), :] = o.astype(o_ref.dtype)
        return carry

    lax.fori_loop(0, nq, finalize, 0, unroll=math.gcd(nq, 4))


def _diff_attention(qkv, lam_vecs, norm_g, batch, seq, tq, lambda_init):
    t = batch * seq
    lam_vecs = [a.reshape(1, DIFF_QKDIM) for a in lam_vecs]
    lam_spec = pl.BlockSpec((1, DIFF_QKDIM), lambda b, h: (0, 0))
    nq = seq // tq
    kcol = DIFF_WIDTH // DIFF_VDIM
    vcol = 2 * DIFF_WIDTH // DIFF_VDIM
    return pl.pallas_call(
        functools.partial(_attn_kernel, tq=tq, rb=math.gcd(ATTN_ROW_BLOCK, tq), nq=nq,
                          lambda_init=lambda_init),
        grid=(batch, DIFF_HEADS),
        in_specs=[
            pl.BlockSpec((seq, DIFF_VDIM), lambda b, h: (b, h)),
            pl.BlockSpec((seq, DIFF_VDIM), lambda b, h: (b, kcol + h)),
            pl.BlockSpec((seq, DIFF_VDIM), lambda b, h: (b, vcol + h)),
            lam_spec, lam_spec, lam_spec, lam_spec,
            pl.BlockSpec((1, DIFF_VDIM), lambda b, h: (0, h)),
        ],
        out_specs=pl.BlockSpec((seq, DIFF_VDIM), lambda b, h: (b, h)),
        out_shape=jax.ShapeDtypeStruct((t, DIFF_WIDTH), BF16),
        scratch_shapes=[
            pltpu.VMEM((2, 2 * tq, LANES), BF16),
            pltpu.VMEM((tq, LANES), BF16),
            pltpu.VMEM((2 * tq, tq), F32),
            pltpu.VMEM((2 * tq, tq), F32),
            pltpu.VMEM((2 * tq, LANES), F32),
            pltpu.VMEM((2 * tq, LANES), F32),
            pltpu.VMEM((2 * tq, LANES), F32),
            pltpu.VMEM((nq, 2 * tq, 2 * DIFF_VDIM), F32),
        ],
        compiler_params=pltpu.CompilerParams(
            dimension_semantics=("arbitrary", "arbitrary"), vmem_limit_bytes=VMEM_LIMIT),
        name="diff_attention",
    )(qkv, qkv, qkv, *lam_vecs, norm_g)


def _inproj_mix_kernel(h_ref, g_ref, w_ref, wg2_ref, bg_ref, gng_ref, pw_ref, ps_ref,
                       qkv_ref, out_ref, state_scr, halo_scr, *, ts, nb):
    st = pl.program_id(1)
    nchunk = ts // CHUNK
    rows = range(nb)

    @pl.when(st == 0)
    def _():
        state_scr[...] = jnp.zeros_like(state_scr)
        halo_scr[:, 0:POOL_HALO, :] = jnp.zeros((nb, POOL_HALO, POOL_WIDTH), F32)

    def dot(a, b):
        return jnp.dot(a, b, preferred_element_type=F32)

    def dot_nt(a, b):
        return lax.dot_general(a, b, (((1,), (1,)), ((), ())), preferred_element_type=F32)

    def dot_tn(a, b):
        return lax.dot_general(a, b, (((0,), (0,)), ((), ())), preferred_element_type=F32)

    h = h_ref[...].reshape(nb * ts, D_MODEL)
    ms = jnp.mean(h * h, axis=-1, keepdims=True)
    u = (h * lax.rsqrt(ms + EPS) * g_ref[...]).astype(BF16)
    rest_all = dot(u, w_ref[:, QKV_W:])
    rest = [rest_all[b * ts:(b + 1) * ts] for b in rows]

    qkv_tiles = iter(range(QKV_W // QKV_TILE))

    def emit_qkv(count=1):
        for _ in range(count):
            c = next(qkv_tiles, None)
            if c is not None:
                cols = slice(c * QKV_TILE, (c + 1) * QKV_TILE)
                y = dot(u, w_ref[:, cols]).astype(BF16)
                for b in rows:
                    qkv_ref[b, :, cols] = y[b * ts:(b + 1) * ts]

    x = [rest[b][:, REST_POOL:REST_POOL + POOL_WIDTH] for b in rows]
    for b in rows:
        halo_scr[b, POOL_HALO:POOL_HALO + ts, :] = x[b]

    def window_sums(b):
        xf = halo_scr[b]
        xa = xf[:, 0:LANES]
        c2 = xa + pltpu.roll(xa, 1, axis=0)
        c4 = c2 + pltpu.roll(c2, 2, axis=0)
        xb = xf[:, LANES:2 * LANES]
        d2 = xb + pltpu.roll(xb, 1, axis=0)
        d4 = d2 + pltpu.roll(d2, 2, axis=0)
        c8 = d4 + pltpu.roll(d4, 4, axis=0)
        c16 = c8 + pltpu.roll(c8, 8, axis=0)
        cut = slice(POOL_HALO, POOL_HALO + ts)
        return xa[cut], xb[cut], c2[cut], c4[cut], c8[cut], c16[cut]

    sums = [window_sums(b) for b in rows]
    tpos = (st * ts + lax.broadcasted_iota(jnp.int32, (ts, LANES), 0) + 1).astype(F32)
    lane = lax.broadcasted_iota(jnp.int32, (ts, LANES), 1)
    first = lane < POOL_GDIM
    inv_a = jnp.where(first, 1.0 / jnp.minimum(tpos, 2.0), 1.0 / jnp.minimum(tpos, 4.0))
    inv_b = jnp.where(first, 1.0 / jnp.minimum(tpos, 8.0), 1.0 / jnp.minimum(tpos, 16.0))

    def pooled_of(xa, xb, c2, c4, c8, c16):
        mean_a = jnp.where(first, c2, c4) * inv_a
        mean_b = jnp.where(first, c8, c16) * inv_b
        return jnp.concatenate([mean_a - xa, mean_b - xb], axis=1).astype(BF16)

    pooled = [pooled_of(*s) for s in sums]
    pw = pw_ref[...]
    y_pool = [dot(p, pw) * ps_ref[...] for p in pooled]

    emit_qkv()
    gq = [rest[b][:, REST_GQ:REST_GQ + GLA_KW] for b in rows]
    gk = [rest[b][:, REST_GK:REST_GK + GLA_KW] for b in rows]
    gv_b = [rest[b][:, REST_GV:REST_GV + GLA_WIDTH].astype(BF16) for b in rows]
    gr = [rest[b][:, REST_GR:REST_GR + GLA_WIDTH] for b in rows]
    gg = [rest[b][:, REST_GG:REST_GG + LANES] for b in rows]

    w_hi, w_lo = _split_bf16(wg2_ref[...])
    w_cat = jnp.concatenate([w_hi, w_lo], axis=1)
    g_split = [_split_bf16(v) for v in gg]
    xg = []
    for hi, lo in g_split:
        hh = dot(hi, w_cat)
        xg.append(hh[:, 0:GLA_KW] + hh[:, GLA_KW:2 * GLA_KW] + dot(lo, w_hi) + bg_ref[...])
    log_a = [(jnp.minimum(v, 0.0) - jnp.log(1.0 + jnp.exp(-jnp.abs(v)))) * (1.0 / GLA_GATE_TAU)
             for v in xg]

    emit_qkv()
    r = lax.broadcasted_iota(jnp.int32, (ts, ts), 0)
    c = lax.broadcasted_iota(jnp.int32, (ts, ts), 1)
    same_chunk = (r // CHUNK) == (c // CHUNK)
    causal = jnp.logical_and(same_chunk, c <= r)
    tri = jnp.where(causal, 1.0, 0.0).astype(BF16)
    blk = jnp.where(same_chunk, 1.0, 0.0).astype(BF16)
    la_cat = [jnp.concatenate(_split_bf16(v), axis=1) for v in log_a]
    bcum2 = [dot(tri, v) for v in la_cat]
    btot2 = [dot(blk, v) for v in la_cat]
    bcum = [v[:, 0:GLA_KW] + v[:, GLA_KW:2 * GLA_KW] for v in bcum2]
    btot = [v[:, 0:GLA_KW] + v[:, GLA_KW:2 * GLA_KW] for v in btot2]

    q_dec_b = [((q * (GLA_KDIM ** -0.5)) * jnp.exp(bc)).astype(BF16) for q, bc in zip(gq, bcum)]
    k_inv = [(k * jnp.exp(-bc)).astype(BF16) for k, bc in zip(gk, bcum)]
    k_end = [(k * jnp.exp(bt - bc)).astype(BF16) for k, bt, bc in zip(gk, btot, bcum)]
    decay = [jnp.exp(bt) for bt in btot]

    emit_qkv()
    klane = lax.broadcasted_iota(jnp.int32, (ts, GLA_KW), 1)
    vlane = lax.broadcasted_iota(jnp.int32, (ts, GLA_WIDTH), 1)
    o = [None] * nb
    for h in range(GLA_HEADS):
        qh = [jnp.where(klane // GLA_KDIM == h, q, jnp.zeros_like(q)) for q in q_dec_b]
        att = [jnp.where(causal, dot_nt(q, k), 0.0).astype(BF16) for q, k in zip(qh, k_inv)]
        vh = [jnp.where(vlane // GLA_VDIM == h, v, jnp.zeros_like(v)) for v in gv_b]
        part = [dot(a, v) for a, v in zip(att, vh)]
        o = part if h == 0 else [a + b for a, b in zip(o, part)]
        if h % 2 == 1:
            emit_qkv()

    srow = lax.broadcasted_iota(jnp.int32, (GLA_WIDTH, GLA_KW), 0)
    scol = lax.broadcasted_iota(jnp.int32, (GLA_WIDTH, GLA_KW), 1)
    head_diag = (srow // GLA_VDIM) == (scol // GLA_KDIM)
    state = [state_scr[b] for b in rows]
    o_inter = [[] for _ in rows]
    for n in range(nchunk):
        sl = slice(n * CHUNK, (n + 1) * CHUNK)
        for b in rows:
            o_inter[b].append(dot_nt(q_dec_b[b][sl], state[b].astype(BF16)))
        kv_t = [dot_tn(gv_b[b][sl], k_end[b][sl]) for b in rows]
        state = [state[b] * decay[b][n * CHUNK:n * CHUNK + 1, :] + jnp.where(head_diag, kv_t[b], 0.0)
                 for b in rows]
    o = [o[b] + jnp.concatenate(o_inter[b], axis=0) for b in rows]

    emit_qkv()
    hr = lax.broadcasted_iota(jnp.int32, (GLA_WIDTH, GLA_WIDTH), 0)
    hc = lax.broadcasted_iota(jnp.int32, (GLA_WIDTH, GLA_WIDTH), 1)
    ones_blk = jnp.where((hr // GLA_VDIM) == (hc // GLA_VDIM), 1.0, 0.0).astype(BF16)
    sq_split = [_split_bf16(v * v) for v in o]
    ms = [(dot(hi, ones_blk) + dot(lo, ones_blk)) * (1.0 / GLA_VDIM) for hi, lo in sq_split]
    o = [v * lax.rsqrt(m + EPS) * gng_ref[...] for v, m in zip(o, ms)]
    silu = [g / (1.0 + jnp.exp(-g)) for g in gr]

    emit_qkv(QKV_W // QKV_TILE)
    for b in rows:
        out_ref[b] = jnp.concatenate([y_pool[b], o[b] * silu[b]], axis=1).astype(out_ref.dtype)
        state_scr[b] = state[b]
        halo_scr[b, 0:POOL_HALO, :] = x[b][ts - POOL_HALO:ts, :]


def _inproj_mix(h, g, w, wg2, bg, gng, pw, ps, batch, seq, ts):
    t = batch * seq
    nb = math.gcd(batch, MIX_BATCH_ROWS)
    qkv, y_pg = pl.pallas_call(
        functools.partial(_inproj_mix_kernel, ts=ts, nb=nb),
        grid=(batch // nb, seq // ts),
        in_specs=[
            pl.BlockSpec((nb, ts, D_MODEL), lambda b, s: (b, s, 0)),
            _resident((1, D_MODEL)),
            _resident((D_MODEL, W_IN_COLS)),
            _resident((LANES, GLA_KW)),
            _resident((1, GLA_KW)),
            _resident((1, GLA_WIDTH)),
            _resident((POOL_WIDTH, POOL_WIDTH)),
            _resident((1, POOL_WIDTH)),
        ],
        out_specs=[
            pl.BlockSpec((nb, ts, QKV_W), lambda b, s: (b, s, 0)),
            pl.BlockSpec((nb, ts, POOL_WIDTH + GLA_WIDTH), lambda b, s: (b, s, 0)),
        ],
        out_shape=[
            jax.ShapeDtypeStruct((batch, seq, QKV_W), BF16),
            jax.ShapeDtypeStruct((batch, seq, POOL_WIDTH + GLA_WIDTH), BF16),
        ],
        scratch_shapes=[
            pltpu.VMEM((nb, GLA_WIDTH, GLA_KW), F32),
            pltpu.VMEM((nb, POOL_HALO + ts, POOL_WIDTH), F32),
        ],
        compiler_params=pltpu.CompilerParams(
            dimension_semantics=("parallel", "arbitrary"), vmem_limit_bytes=VMEM_LIMIT),
        name="inproj_mix",
    )(h.reshape(batch, seq, D_MODEL), g, w, wg2, bg, gng, pw, ps)
    return qkv.reshape(t, QKV_W), y_pg.reshape(t, POOL_WIDTH + GLA_WIDTH)


def _out_mlp_kernel(h_ref, ypg_ref, yd_ref, wo_ref, g2_ref, w1_ref, w2_ref, gf_ref, o_ref,
                    *, final):
    npg = POOL_WIDTH + GLA_WIDTH
    h1 = (h_ref[...]
          + jnp.dot(ypg_ref[...], wo_ref[0:npg, :], preferred_element_type=F32)
          + jnp.dot(yd_ref[...], wo_ref[npg:D_MODEL, :], preferred_element_type=F32))
    ms = jnp.mean(h1 * h1, axis=-1, keepdims=True)
    z = (h1 * lax.rsqrt(ms + EPS) * g2_ref[...]).astype(BF16)
    a = jnp.maximum(jnp.dot(z, w1_ref[...], preferred_element_type=F32), 0.0)
    acc = h1 + jnp.dot((a * a).astype(BF16), w2_ref[...], preferred_element_type=F32)
    if final:
        ms = jnp.mean(acc * acc, axis=-1, keepdims=True)
        acc = acc * lax.rsqrt(ms + EPS) * gf_ref[...]
    o_ref[...] = acc


def _out_mlp(h, ypg, yd, wo, g2, w1, w2, gf, tm, final):
    t = h.shape[0]
    return pl.pallas_call(
        functools.partial(_out_mlp_kernel, final=final),
        grid=(t // tm,),
        in_specs=[
            pl.BlockSpec((tm, D_MODEL), lambda i: (i, 0)),
            pl.BlockSpec((tm, POOL_WIDTH + GLA_WIDTH), lambda i: (i, 0)),
            pl.BlockSpec((tm, DIFF_WIDTH), lambda i: (i, 0)),
            _resident((D_MODEL, D_MODEL)),
            _resident((1, D_MODEL)),
            _resident((D_MODEL, D_FF)),
            _resident((D_FF, D_MODEL)),
            _resident((1, D_MODEL)),
        ],
        out_specs=pl.BlockSpec((tm, D_MODEL), lambda i: (i, 0)),
        out_shape=jax.ShapeDtypeStruct((t, D_MODEL), F32),
        compiler_params=pltpu.CompilerParams(
            dimension_semantics=("parallel",), vmem_limit_bytes=VMEM_LIMIT),
        name="out_mlp",
    )(h, ypg, yd, wo, g2, w1, w2, gf)


def _prep_w_in(w):
    pool = w[:, 0:256]
    dq = w[:, 256:768] * (DIFF_QKDIM ** -0.5 * LOG2_E)
    dk = w[:, 768:1280]
    dv = w[:, 1280:1792]
    gq = w[:, 1792:1920]
    gk = w[:, 1920:2048]
    gv = w[:, 2048:2304]
    gr = w[:, 2304:2560]
    gg = w[:, 2560:2576]
    pad = jnp.zeros((D_MODEL, REST_W - REST_GG - GLA_GATE_RANK), w.dtype)
    return jnp.concatenate([dq, dk, dv, pool, gq, gk, gv, gr, gg, pad], axis=1).astype(BF16)


def _prep_w_out(w):
    return jnp.concatenate([w[0:256], w[768:1024], w[256:768]], axis=0).astype(BF16)


def _prep_pool_w(pw):
    out = jnp.zeros((POOL_WIDTH, POOL_WIDTH), F32)
    for g in range(len(POOL_WINDOWS)):
        out = out.at[g * POOL_GDIM:(g + 1) * POOL_GDIM, g * POOL_GDIM:(g + 1) * POOL_GDIM].set(pw[g])
    return out.astype(BF16)


def _tiles(seq, t):
    tm = math.gcd(t, 512)
    tq = math.gcd(seq, 512)
    ts = math.gcd(seq, 256)
    return tm, tq, ts


def kernel(x, norm1_g, w_in, pool_w, pool_scale, diff_lq1, diff_lk1, diff_lq2, diff_lk2, diff_norm_g, gla_w_gate2, gla_b_gate, gla_norm_g, w_out, norm2_g, w_mlp1, w_mlp2, final_norm_g):
    batch, seq, d = x.shape
    assert d == D_MODEL and seq % CHUNK == 0
    depth = w_in.shape[0]
    t = batch * seq
    tm, tq, ts = _tiles(seq, t)
    h = x.reshape(t, D_MODEL)
    gf = final_norm_g.reshape(1, D_MODEL)
    for l in range(depth):
        lambda_init = 0.8 - 0.6 * math.exp(-0.3 * l)
        wg2 = jnp.zeros((LANES, GLA_KW), F32).at[0:GLA_GATE_RANK].set(gla_w_gate2[l])
        qkv, y_pg = _inproj_mix(h, norm1_g[l].reshape(1, D_MODEL), _prep_w_in(w_in[l]), wg2,
                                gla_b_gate[l].reshape(1, GLA_KW),
                                gla_norm_g[l].reshape(1, GLA_WIDTH), _prep_pool_w(pool_w[l]),
                                pool_scale[l].reshape(1, POOL_WIDTH), batch, seq, ts)
        lam_vecs = (diff_lq1[l], diff_lk1[l], diff_lq2[l], diff_lk2[l])
        y_diff = _diff_attention(qkv, lam_vecs, diff_norm_g[l].reshape(1, DIFF_WIDTH),
                                 batch, seq, tq, lambda_init)
        h = _out_mlp(h, y_pg, y_diff, _prep_w_out(w_out[l]), norm2_g[l].reshape(1, D_MODEL),
                     w_mlp1[l].astype(BF16), w_mlp2[l].astype(BF16), gf, tm,
                     final=(l == depth - 1))
    return h.reshape(batch, seq, D_MODEL)
```
